```python
import math, functools
import jax, jax.numpy as jnp
from jax import lax
import numpy as np

D_MODEL = 1024
BATCH = 16
SEQ = 2048
DEPTH = 2
DEC_BATCH = 8
DEC_SEQ = 32
PAST_LEN = 2048

CHUNK = 64
LEFT_CHUNKS = 8
BAND_PAST = LEFT_CHUNKS * CHUNK
BAND = BAND_PAST + CHUNK
D_ATTN = D_MODEL // 2
HEAD_DIM = 64
N_HEADS = D_ATTN // HEAD_DIM
REL_CLIP = 128
N_REL = 2 * REL_CLIP + 1
D_POOL = D_MODEL // 2
POOL_WINDOWS = (2, 4, 8, 16)
N_POOL_GRP = len(POOL_WINDOWS)
POOL_GRP = D_POOL // N_POOL_GRP
POOL_PAD = max(POOL_WINDOWS) - 1
N_BRANCH = 2
IN_COLS = D_POOL + 3 * D_ATTN + N_BRANCH * D_MODEL
D_FF = 2816
CONV_W = 3
PLE_DIM = 256
EPS = 1e-6
NEG_INF = -1e30

kernel_name = "hybrid_pool_chunkattn_streaming_encoder_step"


def rmsnorm(x, g):
    xf = x.astype(jnp.float32)
    y = xf * lax.rsqrt(jnp.mean(xf * xf, axis=-1, keepdims=True) + EPS)
    return (y * g.astype(jnp.float32)).astype(x.dtype)


def pool_mixer(u, buf, pos0, w_grp, scale):
    B, L, _ = u.shape
    full = jnp.concatenate([buf, u], axis=1)
    ff = full.astype(jnp.float32)
    cs = jnp.concatenate([jnp.zeros_like(ff[:, :1]), jnp.cumsum(ff, axis=1)], axis=1)
    pos = pos0 + jnp.arange(L, dtype=jnp.int32)
    means = []
    for g, w in enumerate(POOL_WINDOWS):
        sl = slice(g * POOL_GRP, (g + 1) * POOL_GRP)
        end = cs[:, POOL_PAD + 1:POOL_PAD + 1 + L, sl]
        start = cs[:, POOL_PAD + 1 - w:POOL_PAD + 1 - w + L, sl]
        cnt = jnp.minimum(pos + 1, w).astype(jnp.float32)[None, :, None]
        means.append((end - start) / cnt)
    d = (jnp.concatenate(means, axis=-1) - u.astype(jnp.float32)).astype(u.dtype)
    d = d.reshape(B, L, N_POOL_GRP, POOL_GRP)
    y = jnp.einsum('blgc,gcd->blgd', d, w_grp).reshape(B, L, D_POOL) * scale
    return y, full[:, -POOL_PAD:]


def band_attend(q, k, v, q_pos, k_pos, k_valid, rel_bias):
    s = jnp.einsum('bqhd,bkhd->bhqk', q, k).astype(jnp.float32) * (HEAD_DIM ** -0.5)
    rel = jnp.clip(q_pos[:, None] - k_pos[None, :], -REL_CLIP, REL_CLIP) + REL_CLIP
    s = s + rel_bias[:, rel].astype(jnp.float32)[None]
    s = jnp.where(k_valid[None, None, None, :], s, NEG_INF)
    p = jax.nn.softmax(s, axis=-1)
    return jnp.einsum('bhqk,bkhd->bqhd', p.astype(v.dtype), v)


def chunk_attention_prompt(q, k, v, rel_bias):
    B, L, H, Dh = q.shape
    n_chunks = L // CHUNK
    pad = jnp.zeros((B, BAND_PAST, H, Dh), k.dtype)
    k_pad = jnp.concatenate([pad, k], axis=1)
    v_pad = jnp.concatenate([pad, v], axis=1)

    def one_chunk(c):
        start = c * CHUNK
        q_c = lax.dynamic_slice_in_dim(q, start, CHUNK, axis=1)
        k_b = lax.dynamic_slice_in_dim(k_pad, start, BAND, axis=1)
        v_b = lax.dynamic_slice_in_dim(v_pad, start, BAND, axis=1)
        q_pos = start + jnp.arange(CHUNK, dtype=jnp.int32)
        k_pos = start - BAND_PAST + jnp.arange(BAND, dtype=jnp.int32)
        return band_attend(q_c, k_b, v_b, q_pos, k_pos, k_pos >= 0, rel_bias)

    out = lax.map(one_chunk, jnp.arange(n_chunks, dtype=jnp.int32))
    out = jnp.moveaxis(out, 0, 1).reshape(B, L, H * Dh)
    keep = min(BAND_PAST, L)
    return out, k[:, L - keep:], v[:, L - keep:]


def chunk_attention_sample(q, k, v, cache_k, cache_v, rel_bias):
    B, L, H, Dh = q.shape
    n_c = cache_k.shape[1]
    k_all = jnp.concatenate([cache_k, k], axis=1)
    v_all = jnp.concatenate([cache_v, v], axis=1)
    q_pos = PAST_LEN + jnp.arange(L, dtype=jnp.int32)
    k_pos = jnp.concatenate([PAST_LEN - n_c + jnp.arange(n_c, dtype=jnp.int32), q_pos])
    out = band_attend(q, k_all, v_all, q_pos, k_pos, k_pos >= 0, rel_bias).reshape(B, L, H * Dh)
    return out, k, v


def conv_ffn(h, buf, w_up, w_dw, b_dw, w_down):
    L = h.shape[1]
    up = h @ w_up
    a, b = jnp.split(up, 2, axis=-1)
    full = jnp.concatenate([buf, a], axis=1)
    conv = b_dw + sum(full[:, t:t + L] * w_dw[t] for t in range(CONV_W))
    y = (jax.nn.gelu(conv) * b) @ w_down
    return y, full[:, -(CONV_W - 1):]


def trunk_layer(x, p_i, pool_buf, conv_buf, pos0, attn_fn, lw):
    (g_mix, w_in, b_gate, w_pool_grp, pool_scale, w_pool_proj, w_attn_proj, w_out,
     g_ffn, w_up, w_dw, b_dw, w_down, g_ple, w_ple, w_ple_gate) = lw
    B, L, _ = x.shape
    h = rmsnorm(x, g_mix)
    z = h @ w_in
    u, q, k, v, gl = jnp.split(
        z, [D_POOL, D_POOL + D_ATTN, D_POOL + 2 * D_ATTN, D_POOL + 3 * D_ATTN], axis=-1)
    g_pool, g_attn = jnp.split(jax.nn.sigmoid(gl + b_gate), 2, axis=-1)
    pool_out, new_pool = pool_mixer(u, pool_buf, pos0, w_pool_grp, pool_scale)
    attn_out, new_k, new_v = attn_fn(q.reshape(B, L, N_HEADS, HEAD_DIM),
                                     k.reshape(B, L, N_HEADS, HEAD_DIM),
                                     v.reshape(B, L, N_HEADS, HEAD_DIM))
    merged = g_pool * (pool_out @ w_pool_proj) + g_attn * (attn_out @ w_attn_proj)
    x = x + merged @ w_out
    ffn_out, new_conv = conv_ffn(rmsnorm(x, g_ffn), conv_buf, w_up, w_dw, b_dw, w_down)
    x = x + ffn_out
    x = x + (p_i @ w_ple) * jax.nn.sigmoid(rmsnorm(x, g_ple) @ w_ple_gate)
    return x, new_pool, new_k, new_v, new_conv


def setup_inputs(seed: int = 0) -> dict:
    key = jax.random.key(seed)
    ks = jax.random.split(key, 32)

    def nrm(k, shape, scale=1.0):
        return jax.random.normal(k, shape, jnp.float32) * scale

    n_cache_att = min(BAND_PAST, PAST_LEN)
    return {
        "x_prompt": nrm(ks[0], (BATCH, SEQ, D_MODEL)),
        "x_sample": nrm(ks[1], (DEC_BATCH, DEC_SEQ, D_MODEL)),
        "cache_pool": nrm(ks[2], (DEPTH, DEC_BATCH, POOL_PAD, D_POOL)),
        "cache_k": nrm(ks[3], (DEPTH, DEC_BATCH, n_cache_att, N_HEADS, HEAD_DIM)),
        "cache_v": nrm(ks[4], (DEPTH, DEC_BATCH, n_cache_att, N_HEADS, HEAD_DIM)),
        "cache_ffn_conv": nrm(ks[5], (DEPTH, DEC_BATCH, CONV_W - 1, D_FF)),
        "p_prompt": nrm(ks[6], (DEPTH, BATCH, SEQ, PLE_DIM)),
        "p_sample": nrm(ks[7], (DEPTH, DEC_BATCH, DEC_SEQ, PLE_DIM)),
        "g_mix": 1.0 + nrm(ks[8], (DEPTH, D_MODEL), 0.02),
        "w_in": nrm(ks[9], (DEPTH, D_MODEL, IN_COLS), D_MODEL ** -0.5),
        "b_gate": nrm(ks[10], (DEPTH, N_BRANCH * D_MODEL), 0.02),
        "w_pool_grp": nrm(ks[11], (DEPTH, N_POOL_GRP, POOL_GRP, POOL_GRP), POOL_GRP ** -0.5),
        "pool_scale": 1.0 + nrm(ks[12], (DEPTH, D_POOL), 0.02),
        "rel_bias": nrm(ks[13], (DEPTH, N_HEADS, N_REL), 0.5),
        "w_pool_proj": nrm(ks[14], (DEPTH, D_POOL, D_MODEL), D_POOL ** -0.5),
        "w_attn_proj": nrm(ks[15], (DEPTH, D_ATTN, D_MODEL), D_ATTN ** -0.5),
        "w_out": nrm(ks[16], (DEPTH, D_MODEL, D_MODEL), D_MODEL ** -0.5),
        "g_ffn": 1.0 + nrm(ks[17], (DEPTH, D_MODEL), 0.02),
        "w_up": nrm(ks[18], (DEPTH, D_MODEL, 2 * D_FF), D_MODEL ** -0.5),
        "w_dw": nrm(ks[19], (DEPTH, CONV_W, D_FF), CONV_W ** -0.5),
        "b_dw": nrm(ks[20], (DEPTH, D_FF), 0.02),
        "w_down": nrm(ks[21], (DEPTH, D_FF, D_MODEL), D_FF ** -0.5),
        "g_ple": 1.0 + nrm(ks[22], (DEPTH, D_MODEL), 0.02),
        "w_ple": nrm(ks[23], (DEPTH, PLE_DIM, D_MODEL), PLE_DIM ** -0.5),
        "w_ple_gate": nrm(ks[24], (DEPTH, D_MODEL, D_MODEL), D_MODEL ** -0.5),
        "g_final": 1.0 + nrm(ks[25], (D_MODEL,), 0.02),
    }


def reference(x_prompt, x_sample, cache_pool, cache_k, cache_v, cache_ffn_conv, p_prompt, p_sample,
              g_mix, w_in, b_gate, w_pool_grp, pool_scale, rel_bias, w_pool_proj, w_attn_proj, w_out,
              g_ffn, w_up, w_dw, b_dw, w_down, g_ple, w_ple, w_ple_gate, g_final):
    xp, xs = x_prompt, x_sample
    B = xp.shape[0]
    pool_p, k_p, v_p, conv_p = [], [], [], []
    pool_s, k_s, v_s, conv_s = [], [], [], []
    for i in range(DEPTH):
        lw = (g_mix[i], w_in[i], b_gate[i], w_pool_grp[i], pool_scale[i], w_pool_proj[i],
              w_attn_proj[i], w_out[i], g_ffn[i], w_up[i], w_dw[i], b_dw[i], w_down[i],
              g_ple[i], w_ple[i], w_ple_gate[i])
        xp, a1, a2, a3, a4 = trunk_layer(
            xp, p_prompt[i],
            jnp.zeros((B, POOL_PAD, D_POOL), xp.dtype),
            jnp.zeros((B, CONV_W - 1, D_FF), xp.dtype),
            0,
            functools.partial(chunk_attention_prompt, rel_bias=rel_bias[i]),
            lw)
        pool_p.append(a1); k_p.append(a2); v_p.append(a3); conv_p.append(a4)
        xs, b1, b2, b3, b4 = trunk_layer(
            xs, p_sample[i], cache_pool[i], cache_ffn_conv[i], PAST_LEN,
            functools.partial(chunk_attention_sample, cache_k=cache_k[i], cache_v=cache_v[i],
                              rel_bias=rel_bias[i]),
            lw)
        pool_s.append(b1); k_s.append(b2); v_s.append(b3); conv_s.append(b4)
    y_prompt = rmsnorm(xp, g_final)
    y_sample = rmsnorm(xs, g_final)
    new_pool_prompt = jnp.stack(pool_p)
    new_k_prompt = jnp.stack(k_p)
    new_v_prompt = jnp.stack(v_p)
    new_conv_prompt = jnp.stack(conv_p)
    new_pool_sample = jnp.stack(pool_s)
    new_k_sample = jnp.stack(k_s)
    new_v_sample = jnp.stack(v_s)
    new_conv_sample = jnp.stack(conv_s)
    return (y_prompt, y_sample, new_pool_prompt, new_k_prompt, new_v_prompt, new_conv_prompt,
            new_pool_sample, new_k_sample, new_v_sample, new_conv_sample)
```

```python
import functools

import jax
import jax.numpy as jnp
from jax import lax
from jax.experimental import pallas as pl
from jax.experimental.pallas import tpu as pltpu

D_MODEL = 1024
CHUNK = 64
BAND_PAST = 512
D_ATTN = 512
PAST_LEN = 2048
HEAD_DIM = 64
N_HEADS = 8
REL_CLIP = 128
D_POOL = 512
POOL_WINDOWS = (2, 4, 8, 16)
POOL_GRP = 128
POOL_PAD = 15
D_FF = 2816
PLE_DIM = 256
EPS = 1e-6
NEG_INF = -1e30

LANES = 128
SUBLANES = 8
MXU_DIM = 256
VMEM_LIMIT_BYTES = 56 * 1024 * 1024

POOL_HALO = 16
CONV_HALO = SUBLANES
FF_CHUNK = MXU_DIM
N_FF_CHUNKS = D_FF // FF_CHUNK
HEADS_PER_VREG = LANES // HEAD_DIM

BF16 = jnp.bfloat16
F32 = jnp.float32


def _rms(x, g):
    return x * lax.rsqrt(jnp.mean(x * x, axis=-1, keepdims=True) + EPS) * g


def _dot(a, b):
    return jnp.dot(a, b, preferred_element_type=F32)


def _const_spec(shape):
    nd = len(shape)
    return pl.BlockSpec(shape, lambda b, t: (0,) * nd, pipeline_mode=pl.Buffered(1))


def _mixer_kernel(*refs, nb, tl, nt, grp, kband, pos0, has_cache):
    it = iter(refs)
    x_ref = next(it)
    if has_cache:
        pool0_ref, k0_ref, v0_ref = next(it), next(it), next(it)
    (gmix_ref, win_ref, bgate_ref, wgrp_ref, pscale_ref, wpp_ref, wap_ref, wout_ref, tab_ref,
     xo_ref, npool_ref, nk_ref, nv_ref,
     h_sc, est, kst, vst, qa_sc, qb_sc, d_sc, att_sc) = it

    m = nb * tl
    t = pl.program_id(1)
    last = t == nt - 1
    srows = kst.shape[1]

    @pl.when(t == 0)
    def _init_state():
        if has_cache:
            est[:, 0:POOL_HALO, :] = pool0_ref[...]
            kst[:, 0:BAND_PAST, :] = k0_ref[...]
            vst[:, 0:BAND_PAST, :] = v0_ref[...]
        else:
            est[:, 0:POOL_HALO, :] = jnp.zeros((nb, POOL_HALO, D_POOL), F32)
            kst[:, 0:BAND_PAST, :] = jnp.zeros((nb, BAND_PAST, D_ATTN), BF16)
            vst[:, 0:BAND_PAST, :] = jnp.zeros((nb, BAND_PAST, D_ATTN), BF16)
        if srows > BAND_PAST + tl:
            pad = srows - BAND_PAST - tl
            kst[:, BAND_PAST + tl:, :] = jnp.zeros((nb, pad, D_ATTN), BF16)
            vst[:, BAND_PAST + tl:, :] = jnp.zeros((nb, pad, D_ATTN), BF16)

    x = x_ref[...].reshape(m, D_MODEL)
    h = _rms(x, gmix_ref[...]).astype(BF16)
    h_sc[...] = h
    u = _dot(h, win_ref[:, 0:D_POOL])
    q = _dot(h, win_ref[:, D_POOL:D_POOL + D_ATTN]) * (HEAD_DIM ** -0.5)
    k = _dot(h, win_ref[:, D_POOL + D_ATTN:D_POOL + 2 * D_ATTN])
    v = _dot(h, win_ref[:, D_POOL + 2 * D_ATTN:D_POOL + 3 * D_ATTN])

    lane = lax.broadcasted_iota(jnp.int32, (m, D_ATTN), 1)
    first_head = (lane % LANES) < HEAD_DIM
    qa_sc[...] = jnp.where(first_head, q, 0.0).astype(BF16)
    qb_sc[...] = jnp.where(first_head, 0.0, q).astype(BF16)

    kst[:, BAND_PAST:BAND_PAST + tl, :] = k.astype(BF16).reshape(nb, tl, D_ATTN)
    vst[:, BAND_PAST:BAND_PAST + tl, :] = v.astype(BF16).reshape(nb, tl, D_ATTN)

    @pl.when(last)
    def _emit_kv():
        nk_ref[...] = k.reshape(nb, tl, D_ATTN)
        nv_ref[...] = v.reshape(nb, tl, D_ATTN)

    row = lax.broadcasted_iota(jnp.int32, (tl, POOL_GRP), 0)
    for b in range(nb):
        est[b, POOL_HALO:POOL_HALO + tl, :] = u[b * tl:(b + 1) * tl, :]
        e = est[b]
        for g, w in enumerate(POOL_WINDOWS):
            eg = e[:, g * POOL_GRP:(g + 1) * POOL_GRP]
            s = eg
            sh = 1
            while sh < w:
                s = s + pltpu.roll(s, sh, 0)
                sh *= 2
            if pos0 >= POOL_PAD:
                mean = s[POOL_HALO:, :] * (1.0 / w)
            else:
                cnt = jnp.minimum(pos0 + t * tl + row + 1, w).astype(F32)
                mean = s[POOL_HALO:, :] / cnt
            dg = mean - eg[POOL_HALO:, :]
            d_sc[b * tl:(b + 1) * tl, g * POOL_GRP:(g + 1) * POOL_GRP] = dg.astype(BF16)
        tail = e[tl:tl + POOL_HALO, :]
        est[b, 0:POOL_HALO, :] = tail

        @pl.when(last)
        def _emit_pool():
            npool_ref[b] = tail

    n_grp = tl // grp
    olane = lax.broadcasted_iota(jnp.int32, (grp, LANES), 1)
    kcol = lax.broadcasted_iota(jnp.int32, (1, kband), 1)
    for b in range(nb):
        for g in range(n_grp):
            r0 = b * tl + g * grp
            if pos0 >= BAND_PAST:
                kmask = None
            else:
                first_pos = pos0 + t * tl + g * grp - BAND_PAST
                kmask = jnp.where(kcol + first_pos >= 0, 0.0, NEG_INF).astype(F32)
            for hp in range(N_HEADS // HEADS_PER_VREG):
                cols = slice(hp * LANES, (hp + 1) * LANES)
                qs = jnp.concatenate([qa_sc[r0:r0 + grp, cols], qb_sc[r0:r0 + grp, cols]], axis=0)
                k2 = kst[b, g * grp:g * grp + kband, cols]
                v2 = vst[b, g * grp:g * grp + kband, cols]
                s = lax.dot_general(qs, k2, (((1,), (1,)), ((), ())), preferred_element_type=F32)
                ps, ls = [], []
                for j in range(HEADS_PER_VREG):
                    sj = s[j * grp:(j + 1) * grp, :] + tab_ref[hp * HEADS_PER_VREG + j]
                    if kmask is not None:
                        sj = sj + kmask
                    mj = jnp.max(sj, axis=-1, keepdims=True)
                    pj = jnp.exp(sj - mj)
                    ls.append(jnp.sum(pj, axis=-1, keepdims=True))
                    ps.append(pj.astype(BF16))
                o = _dot(jnp.concatenate(ps, axis=0), v2)
                oa = o[0:grp, :] / ls[0]
                ob = o[grp:2 * grp, :] / ls[1]
                att_sc[r0:r0 + grp, cols] = jnp.where(olane < HEAD_DIM, oa, ob).astype(BF16)

    ys = [_dot(d_sc[:, g * POOL_GRP:(g + 1) * POOL_GRP], wgrp_ref[g]) for g in range(len(POOL_WINDOWS))]
    y = (jnp.concatenate(ys, axis=1) * pscale_ref[...]).astype(BF16)
    pool_p = _dot(y, wpp_ref[...])
    g_off = D_POOL + 3 * D_ATTN
    gate_p = jax.nn.sigmoid(_dot(h_sc[...], win_ref[:, g_off:g_off + D_MODEL]) + bgate_ref[:, 0:D_MODEL])
    merged = gate_p * pool_p
    attn_p = _dot(att_sc[...], wap_ref[...])
    gate_a = jax.nn.sigmoid(_dot(h_sc[...], win_ref[:, g_off + D_MODEL:g_off + 2 * D_MODEL])
                            + bgate_ref[:, D_MODEL:2 * D_MODEL])
    merged = (merged + gate_a * attn_p).astype(BF16)
    xo = x_ref[...].reshape(m, D_MODEL) + _dot(merged, wout_ref[...])
    xo_ref[...] = xo.reshape(nb, tl, D_MODEL)

    if nt > 1:
        @pl.when(jnp.logical_not(last))
        def _shift_band():
            kst[:, 0:BAND_PAST, :] = kst[:, tl:tl + BAND_PAST, :]
            vst[:, 0:BAND_PAST, :] = vst[:, tl:tl + BAND_PAST, :]


def _band_table(rel_bias, grp, kband):
    i = jnp.arange(grp, dtype=jnp.int32)[:, None]
    j = jnp.arange(kband, dtype=jnp.int32)[None, :]
    lo = (i // CHUNK) * CHUNK
    valid = (j >= lo) & (j < lo + BAND_PAST + CHUNK) & (j < BAND_PAST + grp)
    rel = jnp.clip(i - j + BAND_PAST, -REL_CLIP, REL_CLIP) + REL_CLIP
    return jnp.where(valid[None], rel_bias[:, rel].astype(F32), NEG_INF)


def _mixer(x, cache, lw, *, tl, grp, pos0):
    bsz, seq, _ = x.shape
    has_cache = cache is not None
    nb = bsz if has_cache else 1
    nt = seq // tl
    assert seq % tl == 0 and tl % grp == 0 and tl % (2 * SUBLANES) == 0
    assert not (has_cache and nt != 1)
    kband = -(-(BAND_PAST + grp) // LANES) * LANES
    srows = max(BAND_PAST + tl, (tl // grp - 1) * grp + kband)
    m = nb * tl
    gmix, win, bgate, wgrp, pscale, wpp, wap, wout, rel_bias = lw
    tab = _band_table(rel_bias, grp, kband)

    row_spec = pl.BlockSpec((nb, tl, D_MODEL), lambda b, t: (b, t, 0))
    in_specs = [row_spec]
    args = [x]
    if has_cache:
        in_specs += [pl.BlockSpec((nb, POOL_HALO, D_POOL), lambda b, t: (b, 0, 0)),
                     pl.BlockSpec((nb, BAND_PAST, D_ATTN), lambda b, t: (b, 0, 0)),
                     pl.BlockSpec((nb, BAND_PAST, D_ATTN), lambda b, t: (b, 0, 0))]
        args += list(cache)
    weights = [gmix, win, bgate, wgrp, pscale, wpp, wap, wout, tab]
    in_specs += [_const_spec(w.shape) for w in weights]
    args += weights

    out_shape = [jax.ShapeDtypeStruct((bsz, seq, D_MODEL), F32),
                 jax.ShapeDtypeStruct((bsz, POOL_HALO, D_POOL), F32),
                 jax.ShapeDtypeStruct((bsz, tl, D_ATTN), F32),
                 jax.ShapeDtypeStruct((bsz, tl, D_ATTN), F32)]
    out_specs = [row_spec,
                 pl.BlockSpec((nb, POOL_HALO, D_POOL), lambda b, t: (b, 0, 0)),
                 pl.BlockSpec((nb, tl, D_ATTN), lambda b, t: (b, 0, 0)),
                 pl.BlockSpec((nb, tl, D_ATTN), lambda b, t: (b, 0, 0))]
    scratch = [pltpu.VMEM((m, D_MODEL), BF16),
               pltpu.VMEM((nb, POOL_HALO + tl, D_POOL), F32),
               pltpu.VMEM((nb, srows, D_ATTN), BF16),
               pltpu.VMEM((nb, srows, D_ATTN), BF16),
               pltpu.VMEM((m, D_ATTN), BF16),
               pltpu.VMEM((m, D_ATTN), BF16),
               pltpu.VMEM((m, D_POOL), BF16),
               pltpu.VMEM((m, D_ATTN), BF16)]
    kern = functools.partial(_mixer_kernel, nb=nb, tl=tl, nt=nt, grp=grp, kband=kband,
                             pos0=pos0, has_cache=has_cache)
    return pl.pallas_call(
        kern,
        grid=(bsz // nb, nt),
        in_specs=in_specs,
        out_specs=out_specs,
        out_shape=out_shape,
        scratch_shapes=scratch,
        compiler_params=pltpu.CompilerParams(
            dimension_semantics=("arbitrary", "arbitrary"), vmem_limit_bytes=VMEM_LIMIT_BYTES),
        name="mixer_cached" if has_cache else "mixer",
    )(*args)


def _ffn_kernel(x_ref, p_ref, conv0_ref, gffn_ref, wup_ref, wdw_ref, bdw_ref, wdown_ref,
                gple_ref, wple_ref, wpg_ref, gfin_ref,
                xo_ref, nconv_ref, cst, h_sc, y_sc, *, nb, tl, nt, final_norm):
    m = nb * tl
    t = pl.program_id(1)
    last = t == nt - 1

    @pl.when(t == 0)
    def _init_state():
        cst[...] = conv0_ref[...]

    x = x_ref[...].reshape(m, D_MODEL)
    h_sc[...] = _rms(x, gffn_ref[...]).astype(BF16)
    row = lax.broadcasted_iota(jnp.int32, (tl, FF_CHUNK), 0)
    for j in range(N_FF_CHUNKS):
        cols = slice(j * FF_CHUNK, (j + 1) * FF_CHUNK)
        ab = _dot(h_sc[...], wup_ref[:, 2 * j * FF_CHUNK:2 * (j + 1) * FF_CHUNK])
        w0, w1, w2 = wdw_ref[0:1, cols], wdw_ref[1:2, cols], wdw_ref[2:3, cols]
        bias = bdw_ref[:, cols]
        for b in range(nb):
            a = ab[b * tl:(b + 1) * tl, 0:FF_CHUNK]
            gate = ab[b * tl:(b + 1) * tl, FF_CHUNK:2 * FF_CHUNK]
            c6 = cst[b, CONV_HALO - 2:CONV_HALO - 1, cols]
            c7 = cst[b, CONV_HALO - 1:CONV_HALO, cols]
            a1 = jnp.where(row == 0, c7, pltpu.roll(a, 1, 0))
            a2 = jnp.where(row == 0, c6, jnp.where(row == 1, c7, pltpu.roll(a, 2, 0)))
            conv = bias + a2 * w0 + a1 * w1 + a * w2
            y_sc[b * tl:(b + 1) * tl, cols] = (jax.nn.gelu(conv) * gate).astype(BF16)
            tail = a[tl - CONV_HALO:tl, :]
            cst[b, :, cols] = tail

            @pl.when(last)
            def _emit_conv():
                nconv_ref[b, :, cols] = tail

    x2 = x_ref[...].reshape(m, D_MODEL) + _dot(y_sc[...], wdown_ref[...])
    hp = _rms(x2, gple_ref[...]).astype(BF16)
    gate = jax.nn.sigmoid(_dot(hp, wpg_ref[...]))
    pe = _dot(p_ref[...].reshape(m, PLE_DIM).astype(BF16), wple_ref[...])
    x3 = x2 + pe * gate
    if final_norm:
        x3 = _rms(x3, gfin_ref[...])
    xo_ref[...] = x3.reshape(nb, tl, D_MODEL)


def _ffn(x, p, conv0, lw, *, tl, nb, final_norm):
    bsz, seq, _ = x.shape
    nt = seq // tl
    assert seq % tl == 0 and bsz % nb == 0 and tl % CONV_HALO == 0
    m = nb * tl
    weights = list(lw)
    row_spec = pl.BlockSpec((nb, tl, D_MODEL), lambda b, t: (b, t, 0))
    in_specs = [row_spec,
                pl.BlockSpec((nb, tl, PLE_DIM), lambda b, t: (b, t, 0)),
                pl.BlockSpec((nb, CONV_HALO, D_FF), lambda b, t: (b, 0, 0))]
    in_specs += [_const_spec(w.shape) for w in weights]
    kern = functools.partial(_ffn_kernel, nb=nb, tl=tl, nt=nt, final_norm=final_norm)
    return pl.pallas_call(
        kern,
        grid=(bsz // nb, nt),
        in_specs=in_specs,
        out_specs=[row_spec, pl.BlockSpec((nb, CONV_HALO, D_FF), lambda b, t: (b, 0, 0))],
        out_shape=[jax.ShapeDtypeStruct((bsz, seq, D_MODEL), F32),
                   jax.ShapeDtypeStruct((bsz, CONV_HALO, D_FF), F32)],
        scratch_shapes=[pltpu.VMEM((nb, CONV_HALO, D_FF), F32),
                        pltpu.VMEM((m, D_MODEL), BF16),
                        pltpu.VMEM((m, D_FF), BF16)],
        compiler_params=pltpu.CompilerParams(
            dimension_semantics=("arbitrary", "arbitrary"), vmem_limit_bytes=VMEM_LIMIT_BYTES),
        name="ffn",
    )(x, p, conv0, *weights)


PROMPT_TILE = 512
PROMPT_GROUP = 256


def kernel(x_prompt, x_sample, cache_pool, cache_k, cache_v, cache_ffn_conv, p_prompt, p_sample,
           g_mix, w_in, b_gate, w_pool_grp, pool_scale, rel_bias, w_pool_proj, w_attn_proj, w_out,
           g_ffn, w_up, w_dw, b_dw, w_down, g_ple, w_ple, w_ple_gate, g_final):
    depth = w_in.shape[0]
    bp = x_prompt.shape[0]
    bs, ls, _ = x_sample.shape

    wa = w_up[:, :, :D_FF].reshape(depth, D_MODEL, N_FF_CHUNKS, 1, FF_CHUNK)
    wb = w_up[:, :, D_FF:].reshape(depth, D_MODEL, N_FF_CHUNKS, 1, FF_CHUNK)
    w_up_r = jnp.concatenate([wa, wb], axis=3).reshape(depth, D_MODEL, 2 * D_FF).astype(BF16)
    w_in_b, w_grp_b = w_in.astype(BF16), w_pool_grp.astype(BF16)
    w_pp_b, w_ap_b, w_out_b = w_pool_proj.astype(BF16), w_attn_proj.astype(BF16), w_out.astype(BF16)
    w_down_b, w_ple_b, w_pg_b = w_down.astype(BF16), w_ple.astype(BF16), w_ple_gate.astype(BF16)
    row2 = lambda a: a[:, None, :]
    g_mix2, b_gate2, pscale2, g_ffn2, b_dw2, g_ple2 = map(row2, (g_mix, b_gate, pool_scale, g_ffn, b_dw, g_ple))
    g_fin2 = g_final[None, :]

    pool0_s = jnp.pad(cache_pool, ((0, 0), (0, 0), (POOL_HALO - POOL_PAD, 0), (0, 0)))
    k0_s = cache_k.reshape(depth, bs, BAND_PAST, D_ATTN).astype(BF16)
    v0_s = cache_v.reshape(depth, bs, BAND_PAST, D_ATTN).astype(BF16)
    conv0_s = jnp.pad(cache_ffn_conv, ((0, 0), (0, 0), (CONV_HALO - cache_ffn_conv.shape[2], 0), (0, 0)))
    conv0_p = jnp.zeros((bp, CONV_HALO, D_FF), F32)

    xp, xs = x_prompt, x_sample
    outs_p, outs_s = [], []
    for i in range(depth):
        mix_w = (g_mix2[i], w_in_b[i], b_gate2[i], w_grp_b[i], pscale2[i], w_pp_b[i], w_ap_b[i],
                 w_out_b[i], rel_bias[i])
        ffn_w = (g_ffn2[i], w_up_r[i], w_dw[i], b_dw2[i], w_down_b[i], g_ple2[i], w_ple_b[i], w_pg_b[i],
                 g_fin2)
        fin = i == depth - 1
        xp, npool, nk, nv = _mixer(xp, None, mix_w, tl=PROMPT_TILE, grp=PROMPT_GROUP, pos0=0)
        xp, nconv = _ffn(xp, p_prompt[i], conv0_p, ffn_w, tl=PROMPT_TILE, nb=1, final_norm=fin)
        outs_p.append((npool, nk, nv, nconv))
        xs, npool, nk, nv = _mixer(xs, (pool0_s[i], k0_s[i], v0_s[i]), mix_w, tl=ls, grp=ls, pos0=PAST_LEN)
        xs, nconv = _ffn(xs, p_sample[i], conv0_s[i], ffn_w, tl=ls, nb=bs, final_norm=fin)
        outs_s.append((npool, nk, nv, nconv))

    def assemble(outs, bsz):
        npool = jnp.stack([o[0][:, POOL_HALO - POOL_PAD:, :] for o in outs])
        nk = jnp.stack([o[1].reshape(bsz, -1, N_HEADS, HEAD_DIM) for o in outs])
        nv = jnp.stack([o[2].reshape(bsz, -1, N_HEADS, HEAD_DIM) for o in outs])
        nconv = jnp.stack([o[3][:, CONV_HALO - 2:, :] for o in outs])
        return npool, nk, nv, nconv

    pp, kp, vp, cp = assemble(outs_p, bp)
    ps, ks, vs, cs = assemble(outs_s, bs)
    return (xp, xs, pp, kp, vp, cp, ps, ks, vs, cs)
```

```python
import functools

import jax
import jax.numpy as jnp
import numpy as np
from jax import lax
from jax.experimental import pallas as pl
from jax.experimental.pallas import tpu as pltpu

D_MODEL = 1024
CHUNK = 64
BAND_PAST = 512
D_ATTN = 512
PAST_LEN = 2048
HEAD_DIM = 64
N_HEADS = 8
REL_CLIP = 128
D_POOL = 512
POOL_WINDOWS = (2, 4, 8, 16)
POOL_GRP = 128
POOL_PAD = 15
D_FF = 2816
PLE_DIM = 256
EPS = 1e-6
NEG_INF = -1e30

LANES = 128
SUBLANES = 8
MXU_DIM = 256
VMEM_LIMIT_BYTES = 56 * 1024 * 1024

POOL_HALO = 16
CONV_HALO = SUBLANES
FF_CHUNK = MXU_DIM
N_FF_CHUNKS = D_FF // FF_CHUNK
HEADS_PER_VREG = LANES // HEAD_DIM

BF16 = jnp.bfloat16
F32 = jnp.float32


def _rms(x, g):
    return x * lax.rsqrt(jnp.mean(x * x, axis=-1, keepdims=True) + EPS) * g


def _dot(a, b):
    return jnp.dot(a, b, preferred_element_type=F32)


def _layer_spec(arr, layer):
    shape = arr.shape[1:]
    nd = len(shape)
    return pl.BlockSpec((None,) + shape, lambda b, t: (layer,) + (0,) * nd, pipeline_mode=pl.Buffered(1))


def _mixer_kernel(*refs, nb, tl, nt, grp, kband, pos0, has_cache):
    it = iter(refs)
    x_ref = next(it)
    if has_cache:
        pool0_ref, k0_ref, v0_ref = next(it), next(it), next(it)
    (gmix_ref, win_ref, bgate_ref, wgrp_ref, pscale_ref, wpp_ref, wap_ref, wout_ref, tab_ref,
     xo_ref, npool_ref, nk_ref, nv_ref,
     h_sc, est, kst, vst, qa_sc, qb_sc, d_sc, att_sc) = it

    m = nb * tl
    t = pl.program_id(1)
    last = t == nt - 1
    srows = kst.shape[1]

    @pl.when(t == 0)
    def _init_state():
        if has_cache:
            est[:, 0:POOL_HALO, :] = pool0_ref[...]
            kst[:, 0:BAND_PAST, :] = k0_ref[...]
            vst[:, 0:BAND_PAST, :] = v0_ref[...]
        else:
            est[:, 0:POOL_HALO, :] = jnp.zeros((nb, POOL_HALO, D_POOL), F32)
            kst[:, 0:BAND_PAST, :] = jnp.zeros((nb, BAND_PAST, D_ATTN), BF16)
            vst[:, 0:BAND_PAST, :] = jnp.zeros((nb, BAND_PAST, D_ATTN), BF16)
        if srows > BAND_PAST + tl:
            pad = srows - BAND_PAST - tl
            kst[:, BAND_PAST + tl:, :] = jnp.zeros((nb, pad, D_ATTN), BF16)
            vst[:, BAND_PAST + tl:, :] = jnp.zeros((nb, pad, D_ATTN), BF16)

    x = x_ref[...].reshape(m, D_MODEL)
    h = _rms(x, gmix_ref[...]).astype(BF16)
    h_sc[...] = h
    u = _dot(h, win_ref[:, 0:D_POOL])
    q = _dot(h, win_ref[:, D_POOL:D_POOL + D_ATTN]) * (HEAD_DIM ** -0.5)
    k = _dot(h, win_ref[:, D_POOL + D_ATTN:D_POOL + 2 * D_ATTN])
    v = _dot(h, win_ref[:, D_POOL + 2 * D_ATTN:D_POOL + 3 * D_ATTN])

    lane = lax.broadcasted_iota(jnp.int32, (m, D_ATTN), 1)
    first_head = (lane % LANES) < HEAD_DIM
    qa_sc[...] = jnp.where(first_head, q, 0.0).astype(BF16)
    qb_sc[...] = jnp.where(first_head, 0.0, q).astype(BF16)

    kst[:, BAND_PAST:BAND_PAST + tl, :] = k.astype(BF16).reshape(nb, tl, D_ATTN)
    vst[:, BAND_PAST:BAND_PAST + tl, :] = v.astype(BF16).reshape(nb, tl, D_ATTN)

    @pl.when(last)
    def _emit_kv():
        nk_ref[...] = k.reshape(nb, tl, D_ATTN)
        nv_ref[...] = v.reshape(nb, tl, D_ATTN)

    row = lax.broadcasted_iota(jnp.int32, (tl, POOL_GRP), 0)
    for b in range(nb):
        est[b, POOL_HALO:POOL_HALO + tl, :] = u[b * tl:(b + 1) * tl, :]
        e = est[b]
        for g, w in enumerate(POOL_WINDOWS):
            eg = e[:, g * POOL_GRP:(g + 1) * POOL_GRP]
            s = eg
            sh = 1
            while sh < w:
                s = s + pltpu.roll(s, sh, 0)
                sh *= 2
            if pos0 >= POOL_PAD:
                mean = s[POOL_HALO:, :] * (1.0 / w)
            else:
                cnt = jnp.minimum(pos0 + t * tl + row + 1, w).astype(F32)
                mean = s[POOL_HALO:, :] / cnt
            dg = mean - eg[POOL_HALO:, :]
            d_sc[b * tl:(b + 1) * tl, g * POOL_GRP:(g + 1) * POOL_GRP] = dg.astype(BF16)
        tail = e[tl:tl + POOL_HALO, :]
        est[b, 0:POOL_HALO, :] = tail

        @pl.when(last)
        def _emit_pool():
            npool_ref[b] = tail

    n_grp = tl // grp
    olane = lax.broadcasted_iota(jnp.int32, (grp, LANES), 1)
    kcol = lax.broadcasted_iota(jnp.int32, (1, kband), 1)
    for b in range(nb):
        for g in range(n_grp):
            r0 = b * tl + g * grp
            if pos0 >= BAND_PAST:
                kmask = None
            else:
                first_pos = pos0 + t * tl + g * grp - BAND_PAST
                kmask = jnp.where(kcol + first_pos >= 0, 0.0, NEG_INF).astype(F32)
            for hp in range(N_HEADS // HEADS_PER_VREG):
                cols = slice(hp * LANES, (hp + 1) * LANES)
                qs = jnp.concatenate([qa_sc[r0:r0 + grp, cols], qb_sc[r0:r0 + grp, cols]], axis=0)
                k2 = kst[b, g * grp:g * grp + kband, cols]
                v2 = vst[b, g * grp:g * grp + kband, cols]
                s = lax.dot_general(qs, k2, (((1,), (1,)), ((), ())), preferred_element_type=F32)
                ps, ls = [], []
                for j in range(HEADS_PER_VREG):
                    sj = s[j * grp:(j + 1) * grp, :] + tab_ref[hp * HEADS_PER_VREG + j]
                    if kmask is not None:
                        sj = sj + kmask
                    mj = jnp.max(sj, axis=-1, keepdims=True)
                    pj = jnp.exp(sj - mj)
                    ls.append(jnp.sum(pj, axis=-1, keepdims=True))
                    ps.append(pj.astype(BF16))
                o = _dot(jnp.concatenate(ps, axis=0), v2)
                oa = o[0:grp, :] / ls[0]
                ob = o[grp:2 * grp, :] / ls[1]
                att_sc[r0:r0 + grp, cols] = jnp.where(olane < HEAD_DIM, oa, ob).astype(BF16)

    ys = [_dot(d_sc[:, g * POOL_GRP:(g + 1) * POOL_GRP], wgrp_ref[g]) for g in range(len(POOL_WINDOWS))]
    y = (jnp.concatenate(ys, axis=1) * pscale_ref[...]).astype(BF16)
    pool_p = _dot(y, wpp_ref[...])
    g_off = D_POOL + 3 * D_ATTN
    gate_p = jax.nn.sigmoid(_dot(h_sc[...], win_ref[:, g_off:g_off + D_MODEL]) + bgate_ref[:, 0:D_MODEL])
    merged = gate_p * pool_p
    attn_p = _dot(att_sc[...], wap_ref[...])
    gate_a = jax.nn.sigmoid(_dot(h_sc[...], win_ref[:, g_off + D_MODEL:g_off + 2 * D_MODEL])
                            + bgate_ref[:, D_MODEL:2 * D_MODEL])
    merged = (merged + gate_a * attn_p).astype(BF16)
    xo = x_ref[...].reshape(m, D_MODEL) + _dot(merged, wout_ref[...])
    xo_ref[...] = xo.reshape(nb, tl, D_MODEL)

    if nt > 1:
        @pl.when(jnp.logical_not(last))
        def _shift_band():
            kst[:, 0:BAND_PAST, :] = kst[:, tl:tl + BAND_PAST, :]
            vst[:, 0:BAND_PAST, :] = vst[:, tl:tl + BAND_PAST, :]


def _key_band(grp):
    return -(-(BAND_PAST + grp) // LANES) * LANES


def _band_table(rel_bias, grp):
    kband = _key_band(grp)
    span = grp + kband - 1
    rel = np.clip((grp - 1) - np.arange(span) + BAND_PAST, -REL_CLIP, REL_CLIP) + REL_CLIP
    diag = rel_bias[:, :, rel].astype(F32)
    shifted = jnp.tile(diag, (1, 1, grp + 1))[:, :, :grp * (span + 1)]
    toep = shifted.reshape(diag.shape[0], N_HEADS, grp, span + 1)[:, :, ::-1, :kband]
    i = np.arange(grp)[:, None]
    j = np.arange(kband)[None, :]
    lo = (i // CHUNK) * CHUNK
    valid = (j >= lo) & (j < lo + BAND_PAST + CHUNK) & (j < BAND_PAST + grp)
    return jnp.where(valid[None, None], toep, NEG_INF)


def _mixer(x, cache, lw, layer, *, tl, grp, pos0):
    bsz, seq, _ = x.shape
    has_cache = cache is not None
    nb = bsz if has_cache else 1
    nt = seq // tl
    assert seq % tl == 0 and tl % grp == 0 and tl % (2 * SUBLANES) == 0
    assert not (has_cache and nt != 1)
    kband = _key_band(grp)
    srows = max(BAND_PAST + tl, (tl // grp - 1) * grp + kband)
    m = nb * tl

    row_spec = pl.BlockSpec((nb, tl, D_MODEL), lambda b, t: (b, t, 0))
    in_specs = [row_spec]
    args = [x]
    if has_cache:
        in_specs += [pl.BlockSpec((None, nb, POOL_HALO, D_POOL), lambda b, t: (layer, b, 0, 0)),
                     pl.BlockSpec((None, nb, BAND_PAST, D_ATTN), lambda b, t: (layer, b, 0, 0)),
                     pl.BlockSpec((None, nb, BAND_PAST, D_ATTN), lambda b, t: (layer, b, 0, 0))]
        args += list(cache)
    in_specs += [_layer_spec(w, layer) for w in lw]
    args += list(lw)

    out_shape = [jax.ShapeDtypeStruct((bsz, seq, D_MODEL), F32),
                 jax.ShapeDtypeStruct((bsz, POOL_HALO, D_POOL), F32),
                 jax.ShapeDtypeStruct((bsz, tl, D_ATTN), F32),
                 jax.ShapeDtypeStruct((bsz, tl, D_ATTN), F32)]
    out_specs = [row_spec,
                 pl.BlockSpec((nb, POOL_HALO, D_POOL), lambda b, t: (b, 0, 0)),
                 pl.BlockSpec((nb, tl, D_ATTN), lambda b, t: (b, 0, 0)),
                 pl.BlockSpec((nb, tl, D_ATTN), lambda b, t: (b, 0, 0))]
    scratch = [pltpu.VMEM((m, D_MODEL), BF16),
               pltpu.VMEM((nb, POOL_HALO + tl, D_POOL), F32),
               pltpu.VMEM((nb, srows, D_ATTN), BF16),
               pltpu.VMEM((nb, srows, D_ATTN), BF16),
               pltpu.VMEM((m, D_ATTN), BF16),
               pltpu.VMEM((m, D_ATTN), BF16),
               pltpu.VMEM((m, D_POOL), BF16),
               pltpu.VMEM((m, D_ATTN), BF16)]
    kern = functools.partial(_mixer_kernel, nb=nb, tl=tl, nt=nt, grp=grp, kband=kband,
                             pos0=pos0, has_cache=has_cache)
    return pl.pallas_call(
        kern,
        grid=(bsz // nb, nt),
        in_specs=in_specs,
        out_specs=out_specs,
        out_shape=out_shape,
        scratch_shapes=scratch,
        compiler_params=pltpu.CompilerParams(
            dimension_semantics=("arbitrary", "arbitrary"), vmem_limit_bytes=VMEM_LIMIT_BYTES),
        name="mixer_cached" if has_cache else "mixer",
    )(*args)


def _ffn_kernel(x_ref, p_ref, conv0_ref, gffn_ref, wup_ref, wdw_ref, bdw_ref, wdown_ref,
                gple_ref, wple_ref, wpg_ref, gfin_ref,
                xo_ref, nconv_ref, cst, h_sc, y_sc, *, nb, tl, nt, final_norm):
    m = nb * tl
    t = pl.program_id(1)
    last = t == nt - 1

    @pl.when(t == 0)
    def _init_state():
        cst[...] = conv0_ref[...]

    x = x_ref[...].reshape(m, D_MODEL)
    h_sc[...] = _rms(x, gffn_ref[...]).astype(BF16)
    row = lax.broadcasted_iota(jnp.int32, (tl, FF_CHUNK), 0)
    for j in range(N_FF_CHUNKS):
        cols = slice(j * FF_CHUNK, (j + 1) * FF_CHUNK)
        a_all = _dot(h_sc[...], wup_ref[:, cols])
        g_all = _dot(h_sc[...], wup_ref[:, D_FF + j * FF_CHUNK:D_FF + (j + 1) * FF_CHUNK])
        w0, w1, w2 = wdw_ref[0:1, cols], wdw_ref[1:2, cols], wdw_ref[2:3, cols]
        bias = bdw_ref[:, cols]
        for b in range(nb):
            a = a_all[b * tl:(b + 1) * tl, :]
            gate = g_all[b * tl:(b + 1) * tl, :]
            c6 = cst[b, CONV_HALO - 2:CONV_HALO - 1, cols]
            c7 = cst[b, CONV_HALO - 1:CONV_HALO, cols]
            a1 = jnp.where(row == 0, c7, pltpu.roll(a, 1, 0))
            a2 = jnp.where(row == 0, c6, jnp.where(row == 1, c7, pltpu.roll(a, 2, 0)))
            conv = bias + a2 * w0 + a1 * w1 + a * w2
            y_sc[b * tl:(b + 1) * tl, cols] = (jax.nn.gelu(conv) * gate).astype(BF16)
            tail = a[tl - CONV_HALO:tl, :]
            cst[b, :, cols] = tail

            @pl.when(last)
            def _emit_conv():
                nconv_ref[b, :, cols] = tail

    x2 = x_ref[...].reshape(m, D_MODEL) + _dot(y_sc[...], wdown_ref[...])
    hp = _rms(x2, gple_ref[...]).astype(BF16)
    gate = jax.nn.sigmoid(_dot(hp, wpg_ref[...]))
    pe = _dot(p_ref[...].reshape(m, PLE_DIM).astype(BF16), wple_ref[...])
    x3 = x2 + pe * gate
    if final_norm:
        x3 = _rms(x3, gfin_ref[...])
    xo_ref[...] = x3.reshape(nb, tl, D_MODEL)


def _ffn(x, p, conv0, lw, layer, *, tl, nb, final_norm):
    bsz, seq, _ = x.shape
    nt = seq // tl
    assert seq % tl == 0 and bsz % nb == 0 and tl % CONV_HALO == 0
    m = nb * tl
    weights = list(lw)
    row_spec = pl.BlockSpec((nb, tl, D_MODEL), lambda b, t: (b, t, 0))
    in_specs = [row_spec,
                pl.BlockSpec((None, nb, tl, PLE_DIM), lambda b, t: (layer, b, t, 0)),
                pl.BlockSpec((None, nb, CONV_HALO, D_FF), lambda b, t: (layer, b, 0, 0))]
    in_specs += [_layer_spec(w, layer) for w in weights]
    kern = functools.partial(_ffn_kernel, nb=nb, tl=tl, nt=nt, final_norm=final_norm)
    return pl.pallas_call(
        kern,
        grid=(bsz // nb, nt),
        in_specs=in_specs,
        out_specs=[row_spec, pl.BlockSpec((nb, CONV_HALO, D_FF), lambda b, t: (b, 0, 0))],
        out_shape=[jax.ShapeDtypeStruct((bsz, seq, D_MODEL), F32),
                   jax.ShapeDtypeStruct((bsz, CONV_HALO, D_FF), F32)],
        scratch_shapes=[pltpu.VMEM((nb, CONV_HALO, D_FF), F32),
                        pltpu.VMEM((m, D_MODEL), BF16),
                        pltpu.VMEM((m, D_FF), BF16)],
        compiler_params=pltpu.CompilerParams(
            dimension_semantics=("arbitrary", "arbitrary"), vmem_limit_bytes=VMEM_LIMIT_BYTES),
        name="ffn",
    )(x, p, conv0, *weights)


PROMPT_TILE = 512
PROMPT_GROUP = 256


def kernel(x_prompt, x_sample, cache_pool, cache_k, cache_v, cache_ffn_conv, p_prompt, p_sample,
           g_mix, w_in, b_gate, w_pool_grp, pool_scale, rel_bias, w_pool_proj, w_attn_proj, w_out,
           g_ffn, w_up, w_dw, b_dw, w_down, g_ple, w_ple, w_ple_gate, g_final):
    depth = w_in.shape[0]
    bp = x_prompt.shape[0]
    bs, ls, _ = x_sample.shape

    bf = lambda a: a.astype(BF16)
    row2 = lambda a: a[:, None, :]
    mix_w = [row2(g_mix), bf(w_in), row2(b_gate), bf(w_pool_grp), row2(pool_scale), bf(w_pool_proj),
             bf(w_attn_proj), bf(w_out)]
    ffn_w = [row2(g_ffn), bf(w_up), w_dw, row2(b_dw), bf(w_down), row2(g_ple), bf(w_ple), bf(w_ple_gate),
             jnp.broadcast_to(g_final[None, None, :], (depth, 1, D_MODEL))]
    tab_p = _band_table(rel_bias, PROMPT_GROUP)
    tab_s = _band_table(rel_bias, ls)

    pool0_s = jnp.pad(cache_pool, ((0, 0), (0, 0), (POOL_HALO - POOL_PAD, 0), (0, 0)))
    k0_s = bf(cache_k.reshape(depth, bs, BAND_PAST, D_ATTN))
    v0_s = bf(cache_v.reshape(depth, bs, BAND_PAST, D_ATTN))
    conv0_s = jnp.pad(cache_ffn_conv, ((0, 0), (0, 0), (CONV_HALO - cache_ffn_conv.shape[2], 0), (0, 0)))
    conv0_p = jnp.zeros((depth, bp, CONV_HALO, D_FF), F32)

    xp, xs = x_prompt, x_sample
    outs_p, outs_s = [], []
    for i in range(depth):
        fin = i == depth - 1
        xp, npool, nk, nv = _mixer(xp, None, mix_w + [tab_p], i, tl=PROMPT_TILE, grp=PROMPT_GROUP, pos0=0)
        xp, nconv = _ffn(xp, p_prompt, conv0_p, ffn_w, i, tl=PROMPT_TILE, nb=1, final_norm=fin)
        outs_p.append((npool, nk, nv, nconv))
        xs, npool, nk, nv = _mixer(xs, (pool0_s, k0_s, v0_s), mix_w + [tab_s], i, tl=ls, grp=ls, pos0=PAST_LEN)
        xs, nconv = _ffn(xs, p_sample, conv0_s, ffn_w, i, tl=ls, nb=bs, final_norm=fin)
        outs_s.append((npool, nk, nv, nconv))

    def assemble(outs, bsz):
        npool = jnp.stack([o[0][:, POOL_HALO - POOL_PAD:, :] for o in outs])
        nk = jnp.stack([o[1].reshape(bsz, -1, N_HEADS, HEAD_DIM) for o in outs])
        nv = jnp.stack([o[2].reshape(bsz, -1, N_HEADS, HEAD_DIM) for o in outs])
        nconv = jnp.stack([o[3][:, CONV_HALO - 2:, :] for o in outs])
        return npool, nk, nv, nconv

    pp, kp, vp, cp = assemble(outs_p, bp)
    ps, ks, vs, cs = assemble(outs_s, bs)
    return (xp, xs, pp, kp, vp, cp, ps, ks, vs, cs)
```

```python
import functools

import jax
import jax.numpy as jnp
import numpy as np
from jax import lax
from jax.experimental import pallas as pl
from jax.experimental.pallas import tpu as pltpu

D_MODEL = 1024
CHUNK = 64
BAND_PAST = 512
D_ATTN = 512
PAST_LEN = 2048
HEAD_DIM = 64
N_HEADS = 8
REL_CLIP = 128
D_POOL = 512
POOL_WINDOWS = (2, 4, 8, 16)
POOL_GRP = 128
POOL_PAD = 15
D_FF = 2816
PLE_DIM = 256
EPS = 1e-6
NEG_INF = -1e30

LANES = 128
SUBLANES = 8
MXU_DIM = 256
VMEM_LIMIT_BYTES = 56 * 1024 * 1024

POOL_HALO = 16
CONV_HALO = SUBLANES
FF_CHUNK = MXU_DIM
N_FF_CHUNKS = D_FF // FF_CHUNK
HEADS_PER_VREG = LANES // HEAD_DIM

BF16 = jnp.bfloat16
F32 = jnp.float32


def _rms(x, g):
    return x * lax.rsqrt(jnp.mean(x * x, axis=-1, keepdims=True) + EPS) * g


def _dot(a, b):
    return jnp.dot(a, b, preferred_element_type=F32)


def _layer_spec(arr, layer):
    shape = arr.shape[1:]
    nd = len(shape)
    return pl.BlockSpec((None,) + shape, lambda b, t: (layer,) + (0,) * nd, pipeline_mode=pl.Buffered(1))


def _mixer_kernel(*refs, nb, tl, nt, grp, kband, pos0, has_cache):
    it = iter(refs)
    x_ref = next(it)
    if has_cache:
        pool0_ref, k0_ref, v0_ref = next(it), next(it), next(it)
    (gmix_ref, win_ref, bgate_ref, wgrp_ref, pscale_ref, wpp_ref, wap_ref, wout_ref, tab_ref,
     xo_ref, npool_ref, nk_ref, nv_ref,
     h_sc, est, kst, vst, qa_sc, qb_sc, d_sc, att_sc) = it

    m = nb * tl
    t = pl.program_id(1)
    srows = kst.shape[1]

    @pl.when(t == 0)
    def _init_state():
        if has_cache:
            est[:, 0:POOL_HALO, :] = pool0_ref[...]
            kst[:, 0:BAND_PAST, :] = k0_ref[...]
            vst[:, 0:BAND_PAST, :] = v0_ref[...]
        else:
            est[:, 0:POOL_HALO, :] = jnp.zeros((nb, POOL_HALO, D_POOL), F32)
            kst[:, 0:BAND_PAST, :] = jnp.zeros((nb, BAND_PAST, D_ATTN), BF16)
            vst[:, 0:BAND_PAST, :] = jnp.zeros((nb, BAND_PAST, D_ATTN), BF16)
        if srows > BAND_PAST + tl:
            pad = srows - BAND_PAST - tl
            kst[:, BAND_PAST + tl:, :] = jnp.zeros((nb, pad, D_ATTN), BF16)
            vst[:, BAND_PAST + tl:, :] = jnp.zeros((nb, pad, D_ATTN), BF16)

    x = x_ref[...].reshape(m, D_MODEL)
    h = _rms(x, gmix_ref[...]).astype(BF16)
    h_sc[...] = h
    u = _dot(h, win_ref[:, 0:D_POOL])
    q = _dot(h, win_ref[:, D_POOL:D_POOL + D_ATTN]) * (HEAD_DIM ** -0.5)
    k = _dot(h, win_ref[:, D_POOL + D_ATTN:D_POOL + 2 * D_ATTN])
    v = _dot(h, win_ref[:, D_POOL + 2 * D_ATTN:D_POOL + 3 * D_ATTN])

    lane = lax.broadcasted_iota(jnp.int32, (m, D_ATTN), 1)
    first_head = (lane % LANES) < HEAD_DIM
    qa_sc[...] = jnp.where(first_head, q, 0.0).astype(BF16)
    qb_sc[...] = jnp.where(first_head, 0.0, q).astype(BF16)

    kst[:, BAND_PAST:BAND_PAST + tl, :] = k.astype(BF16).reshape(nb, tl, D_ATTN)
    vst[:, BAND_PAST:BAND_PAST + tl, :] = v.astype(BF16).reshape(nb, tl, D_ATTN)

    nk_ref[...] = k.reshape(nb, tl, D_ATTN)
    nv_ref[...] = v.reshape(nb, tl, D_ATTN)

    row = lax.broadcasted_iota(jnp.int32, (tl, POOL_GRP), 0)
    for b in range(nb):
        est[b, POOL_HALO:POOL_HALO + tl, :] = u[b * tl:(b + 1) * tl, :]
        e = est[b]
        for g, w in enumerate(POOL_WINDOWS):
            eg = e[:, g * POOL_GRP:(g + 1) * POOL_GRP]
            s = eg
            sh = 1
            while sh < w:
                s = s + pltpu.roll(s, sh, 0)
                sh *= 2
            if pos0 >= POOL_PAD:
                mean = s[POOL_HALO:, :] * (1.0 / w)
            else:
                cnt = jnp.minimum(pos0 + t * tl + row + 1, w).astype(F32)
                mean = s[POOL_HALO:, :] / cnt
            dg = mean - eg[POOL_HALO:, :]
            d_sc[b * tl:(b + 1) * tl, g * POOL_GRP:(g + 1) * POOL_GRP] = dg.astype(BF16)
        tail = e[tl:tl + POOL_HALO, :]
        est[b, 0:POOL_HALO, :] = tail
        npool_ref[b] = tail

    n_grp = tl // grp
    olane = lax.broadcasted_iota(jnp.int32, (grp, LANES), 1)
    kcol = lax.broadcasted_iota(jnp.int32, (1, kband), 1)
    for b in range(nb):
        for g in range(n_grp):
            r0 = b * tl + g * grp
            if pos0 >= BAND_PAST:
                kmask = None
            else:
                first_pos = pos0 + t * tl + g * grp - BAND_PAST
                kmask = jnp.where(kcol + first_pos >= 0, 0.0, NEG_INF).astype(F32)
            for hp in range(N_HEADS // HEADS_PER_VREG):
                cols = slice(hp * LANES, (hp + 1) * LANES)
                qs = jnp.concatenate([qa_sc[r0:r0 + grp, cols], qb_sc[r0:r0 + grp, cols]], axis=0)
                k2 = kst[b, g * grp:g * grp + kband, cols]
                v2 = vst[b, g * grp:g * grp + kband, cols]
                s = lax.dot_general(qs, k2, (((1,), (1,)), ((), ())), preferred_element_type=F32)
                ps, ls = [], []
                for j in range(HEADS_PER_VREG):
                    sj = s[j * grp:(j + 1) * grp, :] + tab_ref[hp * HEADS_PER_VREG + j]
                    if kmask is not None:
                        sj = sj + kmask
                    mj = jnp.max(sj, axis=-1, keepdims=True)
                    pj = jnp.exp(sj - mj)
                    ls.append(jnp.sum(pj, axis=-1, keepdims=True))
                    ps.append(pj.astype(BF16))
                o = _dot(jnp.concatenate(ps, axis=0), v2)
                oa = o[0:grp, :] / ls[0]
                ob = o[grp:2 * grp, :] / ls[1]
                att_sc[r0:r0 + grp, cols] = jnp.where(olane < HEAD_DIM, oa, ob).astype(BF16)

    ys = [_dot(d_sc[:, g * POOL_GRP:(g + 1) * POOL_GRP], wgrp_ref[g]) for g in range(len(POOL_WINDOWS))]
    y = (jnp.concatenate(ys, axis=1) * pscale_ref[...]).astype(BF16)
    pool_p = _dot(y, wpp_ref[...])
    g_off = D_POOL + 3 * D_ATTN
    gate_p = jax.nn.sigmoid(_dot(h_sc[...], win_ref[:, g_off:g_off + D_MODEL]) + bgate_ref[:, 0:D_MODEL])
    merged = gate_p * pool_p
    attn_p = _dot(att_sc[...], wap_ref[...])
    gate_a = jax.nn.sigmoid(_dot(h_sc[...], win_ref[:, g_off + D_MODEL:g_off + 2 * D_MODEL])
                            + bgate_ref[:, D_MODEL:2 * D_MODEL])
    merged = (merged + gate_a * attn_p).astype(BF16)
    xo = x_ref[...].reshape(m, D_MODEL) + _dot(merged, wout_ref[...])
    xo_ref[...] = xo.reshape(nb, tl, D_MODEL)

    if nt > 1:
        kst[:, 0:BAND_PAST, :] = kst[:, tl:tl + BAND_PAST, :]
        vst[:, 0:BAND_PAST, :] = vst[:, tl:tl + BAND_PAST, :]


def _key_band(grp):
    return -(-(BAND_PAST + grp) // LANES) * LANES


def _band_table(rel_bias, grp):
    kband = _key_band(grp)
    span = grp + kband - 1
    rel = np.clip((grp - 1) - np.arange(span) + BAND_PAST, -REL_CLIP, REL_CLIP) + REL_CLIP
    diag = rel_bias[:, :, rel].astype(F32)
    shifted = jnp.tile(diag, (1, 1, grp + 1))[:, :, grp - 1:grp - 1 + grp * (span - 1)]
    toep = shifted.reshape(diag.shape[0], N_HEADS, grp, span - 1)[:, :, :, :kband]
    i = np.arange(grp)[:, None]
    j = np.arange(kband)[None, :]
    lo = (i // CHUNK) * CHUNK
    valid = (j >= lo) & (j < lo + BAND_PAST + CHUNK) & (j < BAND_PAST + grp)
    return jnp.where(valid[None, None], toep, NEG_INF)


def _mixer(x, cache, lw, layer, *, tl, grp, pos0):
    bsz, seq, _ = x.shape
    has_cache = cache is not None
    nb = bsz if has_cache else 1
    nt = seq // tl
    assert seq % tl == 0 and tl % grp == 0 and tl % (2 * SUBLANES) == 0
    assert not (has_cache and nt != 1)
    kband = _key_band(grp)
    srows = max(BAND_PAST + tl, (tl // grp - 1) * grp + kband)
    m = nb * tl

    row_spec = pl.BlockSpec((nb, tl, D_MODEL), lambda b, t: (b, t, 0))
    in_specs = [row_spec]
    args = [x]
    if has_cache:
        in_specs += [pl.BlockSpec((None, nb, POOL_HALO, D_POOL), lambda b, t: (layer, b, 0, 0)),
                     pl.BlockSpec((None, nb, BAND_PAST, D_ATTN), lambda b, t: (layer, b, 0, 0)),
                     pl.BlockSpec((None, nb, BAND_PAST, D_ATTN), lambda b, t: (layer, b, 0, 0))]
        args += list(cache)
    in_specs += [_layer_spec(w, layer) for w in lw]
    args += list(lw)

    out_shape = [jax.ShapeDtypeStruct((bsz, seq, D_MODEL), F32),
                 jax.ShapeDtypeStruct((bsz, POOL_HALO, D_POOL), F32),
                 jax.ShapeDtypeStruct((bsz, tl, D_ATTN), F32),
                 jax.ShapeDtypeStruct((bsz, tl, D_ATTN), F32)]
    out_specs = [row_spec,
                 pl.BlockSpec((nb, POOL_HALO, D_POOL), lambda b, t: (b, 0, 0)),
                 pl.BlockSpec((nb, tl, D_ATTN), lambda b, t: (b, 0, 0)),
                 pl.BlockSpec((nb, tl, D_ATTN), lambda b, t: (b, 0, 0))]
    scratch = [pltpu.VMEM((m, D_MODEL), BF16),
               pltpu.VMEM((nb, POOL_HALO + tl, D_POOL), F32),
               pltpu.VMEM((nb, srows, D_ATTN), BF16),
               pltpu.VMEM((nb, srows, D_ATTN), BF16),
               pltpu.VMEM((m, D_ATTN), BF16),
               pltpu.VMEM((m, D_ATTN), BF16),
               pltpu.VMEM((m, D_POOL), BF16),
               pltpu.VMEM((m, D_ATTN), BF16)]
    kern = functools.partial(_mixer_kernel, nb=nb, tl=tl, nt=nt, grp=grp, kband=kband,
                             pos0=pos0, has_cache=has_cache)
    return pl.pallas_call(
        kern,
        grid=(bsz // nb, nt),
        in_specs=in_specs,
        out_specs=out_specs,
        out_shape=out_shape,
        scratch_shapes=scratch,
        compiler_params=pltpu.CompilerParams(
            dimension_semantics=("arbitrary", "arbitrary"), vmem_limit_bytes=VMEM_LIMIT_BYTES),
        name="mixer_cached" if has_cache else "mixer",
    )(*args)


def _ffn_kernel(x_ref, p_ref, conv0_ref, gffn_ref, wup_ref, wdw_ref, bdw_ref, wdown_ref,
                gple_ref, wple_ref, wpg_ref, gfin_ref,
                xo_ref, nconv_ref, cst, h_sc, y_sc, *, nb, tl, final_norm):
    m = nb * tl
    t = pl.program_id(1)

    @pl.when(t == 0)
    def _init_state():
        cst[...] = conv0_ref[...]

    x = x_ref[...].reshape(m, D_MODEL)
    h_sc[...] = _rms(x, gffn_ref[...]).astype(BF16)
    row = lax.broadcasted_iota(jnp.int32, (tl, FF_CHUNK), 0)
    for j in range(N_FF_CHUNKS):
        cols = slice(j * FF_CHUNK, (j + 1) * FF_CHUNK)
        a_all = _dot(h_sc[...], wup_ref[:, cols])
        g_all = _dot(h_sc[...], wup_ref[:, D_FF + j * FF_CHUNK:D_FF + (j + 1) * FF_CHUNK])
        w0, w1, w2 = wdw_ref[0:1, cols], wdw_ref[1:2, cols], wdw_ref[2:3, cols]
        bias = bdw_ref[:, cols]
        for b in range(nb):
            a = a_all[b * tl:(b + 1) * tl, :]
            gate = g_all[b * tl:(b + 1) * tl, :]
            c6 = cst[b, CONV_HALO - 2:CONV_HALO - 1, cols]
            c7 = cst[b, CONV_HALO - 1:CONV_HALO, cols]
            a1 = jnp.where(row == 0, c7, pltpu.roll(a, 1, 0))
            a2 = jnp.where(row == 0, c6, jnp.where(row == 1, c7, pltpu.roll(a, 2, 0)))
            conv = bias + a2 * w0 + a1 * w1 + a * w2
            y_sc[b * tl:(b + 1) * tl, cols] = (jax.nn.gelu(conv) * gate).astype(BF16)
            tail = a[tl - CONV_HALO:tl, :]
            cst[b, :, cols] = tail
            nconv_ref[b, :, cols] = tail

    x2 = x_ref[...].reshape(m, D_MODEL) + _dot(y_sc[...], wdown_ref[...])
    hp = _rms(x2, gple_ref[...]).astype(BF16)
    gate = jax.nn.sigmoid(_dot(hp, wpg_ref[...]))
    pe = _dot(p_ref[...].reshape(m, PLE_DIM).astype(BF16), wple_ref[...])
    x3 = x2 + pe * gate
    if final_norm:
        x3 = _rms(x3, gfin_ref[...])
    xo_ref[...] = x3.reshape(nb, tl, D_MODEL)


def _ffn(x, p, conv0, lw, layer, *, tl, nb, final_norm):
    bsz, seq, _ = x.shape
    nt = seq // tl
    assert seq % tl == 0 and bsz % nb == 0 and tl % CONV_HALO == 0
    m = nb * tl
    weights = list(lw)
    row_spec = pl.BlockSpec((nb, tl, D_MODEL), lambda b, t: (b, t, 0))
    in_specs = [row_spec,
                pl.BlockSpec((None, nb, tl, PLE_DIM), lambda b, t: (layer, b, t, 0)),
                pl.BlockSpec((None, nb, CONV_HALO, D_FF), lambda b, t: (layer, b, 0, 0))]
    in_specs += [_layer_spec(w, layer) for w in weights]
    kern = functools.partial(_ffn_kernel, nb=nb, tl=tl, final_norm=final_norm)
    return pl.pallas_call(
        kern,
        grid=(bsz // nb, nt),
        in_specs=in_specs,
        out_specs=[row_spec, pl.BlockSpec((nb, CONV_HALO, D_FF), lambda b, t: (b, 0, 0))],
        out_shape=[jax.ShapeDtypeStruct((bsz, seq, D_MODEL), F32),
                   jax.ShapeDtypeStruct((bsz, CONV_HALO, D_FF), F32)],
        scratch_shapes=[pltpu.VMEM((nb, CONV_HALO, D_FF), F32),
                        pltpu.VMEM((m, D_MODEL), BF16),
                        pltpu.VMEM((m, D_FF), BF16)],
        compiler_params=pltpu.CompilerParams(
            dimension_semantics=("arbitrary", "arbitrary"), vmem_limit_bytes=VMEM_LIMIT_BYTES),
        name="ffn",
    )(x, p, conv0, *weights)


PROMPT_TILE = 512
PROMPT_GROUP = 256


def kernel(x_prompt, x_sample, cache_pool, cache_k, cache_v, cache_ffn_conv, p_prompt, p_sample,
           g_mix, w_in, b_gate, w_pool_grp, pool_scale, rel_bias, w_pool_proj, w_attn_proj, w_out,
           g_ffn, w_up, w_dw, b_dw, w_down, g_ple, w_ple, w_ple_gate, g_final):
    depth = w_in.shape[0]
    bp = x_prompt.shape[0]
    bs, ls, _ = x_sample.shape

    bf = lambda a: a.astype(BF16)
    row2 = lambda a: a[:, None, :]
    mix_w = [row2(g_mix), bf(w_in), row2(b_gate), bf(w_pool_grp), row2(pool_scale), bf(w_pool_proj),
             bf(w_attn_proj), bf(w_out)]
    ffn_w = [row2(g_ffn), bf(w_up), w_dw, row2(b_dw), bf(w_down), row2(g_ple), bf(w_ple), bf(w_ple_gate),
             jnp.broadcast_to(g_final[None, None, :], (depth, 1, D_MODEL))]
    tab_p = _band_table(rel_bias, PROMPT_GROUP)
    tab_s = _band_table(rel_bias, ls)

    pool0_s = jnp.pad(cache_pool, ((0, 0), (0, 0), (POOL_HALO - POOL_PAD, 0), (0, 0)))
    k0_s = bf(cache_k.reshape(depth, bs, BAND_PAST, D_ATTN))
    v0_s = bf(cache_v.reshape(depth, bs, BAND_PAST, D_ATTN))
    conv0_s = jnp.pad(cache_ffn_conv, ((0, 0), (0, 0), (CONV_HALO - cache_ffn_conv.shape[2], 0), (0, 0)))
    conv0_p = jnp.zeros((depth, bp, CONV_HALO, D_FF), F32)

    xp, xs = x_prompt, x_sample
    outs_p, outs_s = [], []
    for i in range(depth):
        fin = i == depth - 1
        xp, npool, nk, nv = _mixer(xp, None, mix_w + [tab_p], i, tl=PROMPT_TILE, grp=PROMPT_GROUP, pos0=0)
        xp, nconv = _ffn(xp, p_prompt, conv0_p, ffn_w, i, tl=PROMPT_TILE, nb=1, final_norm=fin)
        outs_p.append((npool, nk, nv, nconv))
        xs, npool, nk, nv = _mixer(xs, (pool0_s, k0_s, v0_s), mix_w + [tab_s], i, tl=ls, grp=ls, pos0=PAST_LEN)
        xs, nconv = _ffn(xs, p_sample, conv0_s, ffn_w, i, tl=ls, nb=bs, final_norm=fin)
        outs_s.append((npool, nk, nv, nconv))

    def assemble(outs, bsz):
        npool = jnp.stack([o[0][:, POOL_HALO - POOL_PAD:, :] for o in outs])
        nk = jnp.stack([o[1].reshape(bsz, -1, N_HEADS, HEAD_DIM) for o in outs])
        nv = jnp.stack([o[2].reshape(bsz, -1, N_HEADS, HEAD_DIM) for o in outs])
        nconv = jnp.stack([o[3][:, CONV_HALO - 2:, :] for o in outs])
        return npool, nk, nv, nconv

    pp, kp, vp, cp = assemble(outs_p, bp)
    ps, ks, vs, cs = assemble(outs_s, bs)
    return (xp, xs, pp, kp, vp, cp, ps, ks, vs, cs)
```

```python
import functools
import math

import jax
import jax.numpy as jnp
import numpy as np
from jax import lax
from jax.experimental import pallas as pl
from jax.experimental.pallas import tpu as pltpu

D_MODEL = 1024
CHUNK = 64
BAND_PAST = 512
D_ATTN = 512
PAST_LEN = 2048
HEAD_DIM = 64
N_HEADS = 8
REL_CLIP = 128
D_POOL = 512
POOL_WINDOWS = (2, 4, 8, 16)
POOL_GRP = 128
POOL_PAD = 15
D_FF = 2816
PLE_DIM = 256
EPS = 1e-6
NEG_INF = -1e30
LOG2E = math.log2(math.e)

LANES = 128
SUBLANES = 8
MXU_DIM = 256
VMEM_LIMIT_BYTES = 56 * 1024 * 1024

POOL_HALO = 16
CONV_HALO = SUBLANES
FF_CHUNK = MXU_DIM
N_FF_CHUNKS = D_FF // FF_CHUNK
HEADS_PER_VREG = LANES // HEAD_DIM

BF16 = jnp.bfloat16
F32 = jnp.float32


def _rms(x, g):
    return x * lax.rsqrt(jnp.mean(x * x, axis=-1, keepdims=True) + EPS) * g


def _dot(a, b):
    return jnp.dot(a, b, preferred_element_type=F32)


def _layer_spec(arr, layer):
    shape = arr.shape[1:]
    nd = len(shape)
    return pl.BlockSpec((None,) + shape, lambda b, t: (layer,) + (0,) * nd, pipeline_mode=pl.Buffered(1))


def _mixer_kernel(*refs, nb, tl, nt, grp, kband, pos0, has_cache):
    it = iter(refs)
    x_ref = next(it)
    if has_cache:
        pool0_ref, k0_ref, v0_ref = next(it), next(it), next(it)
    (gmix_ref, win_ref, bgate_ref, wgrp_ref, pscale_ref, wpp_ref, wap_ref, wout_ref, diag_ref,
     xo_ref, npool_ref, nk_ref, nv_ref,
     h_sc, est, kst, vst, qa_sc, qb_sc, d_sc, att_sc, tab_ref) = it

    m = nb * tl
    t = pl.program_id(1)
    srows = kst.shape[1]

    @pl.when((pl.program_id(0) == 0) & (t == 0))
    def _build_band_table():
        width = diag_ref.shape[1]
        i = lax.broadcasted_iota(jnp.int32, (grp, kband), 0)
        j = lax.broadcasted_iota(jnp.int32, (grp, kband), 1)
        lo = jnp.bitwise_and(i, -CHUNK)
        valid = (j >= lo) & (j < lo + BAND_PAST + CHUNK) & (j < BAND_PAST + grp)
        for hd in range(N_HEADS):
            d = jnp.broadcast_to(diag_ref[hd:hd + 1, :], (grp, width))
            shifted = pltpu.roll(d, width - (grp - 1), 1, stride=1, stride_axis=0)
            tab_ref[hd] = jnp.where(valid, shifted[:, 0:kband], NEG_INF)

    @pl.when(t == 0)
    def _init_state():
        if has_cache:
            est[:, 0:POOL_HALO, :] = pool0_ref[...]
            kst[:, 0:BAND_PAST, :] = k0_ref[...]
            vst[:, 0:BAND_PAST, :] = v0_ref[...]
        else:
            est[:, 0:POOL_HALO, :] = jnp.zeros((nb, POOL_HALO, D_POOL), F32)
            kst[:, 0:BAND_PAST, :] = jnp.zeros((nb, BAND_PAST, D_ATTN), BF16)
            vst[:, 0:BAND_PAST, :] = jnp.zeros((nb, BAND_PAST, D_ATTN), BF16)
        if srows > BAND_PAST + tl:
            pad = srows - BAND_PAST - tl
            kst[:, BAND_PAST + tl:, :] = jnp.zeros((nb, pad, D_ATTN), BF16)
            vst[:, BAND_PAST + tl:, :] = jnp.zeros((nb, pad, D_ATTN), BF16)

    x = x_ref[...].reshape(m, D_MODEL)
    h = _rms(x, gmix_ref[...]).astype(BF16)
    h_sc[...] = h
    u = _dot(h, win_ref[:, 0:D_POOL])
    q = _dot(h, win_ref[:, D_POOL:D_POOL + D_ATTN]) * (HEAD_DIM ** -0.5 * LOG2E)
    k = _dot(h, win_ref[:, D_POOL + D_ATTN:D_POOL + 2 * D_ATTN])
    v = _dot(h, win_ref[:, D_POOL + 2 * D_ATTN:D_POOL + 3 * D_ATTN])

    lane = lax.broadcasted_iota(jnp.int32, (m, D_ATTN), 1)
    first_head = (lane % LANES) < HEAD_DIM
    qa_sc[...] = jnp.where(first_head, q, 0.0).astype(BF16)
    qb_sc[...] = jnp.where(first_head, 0.0, q).astype(BF16)

    kst[:, BAND_PAST:BAND_PAST + tl, :] = k.astype(BF16).reshape(nb, tl, D_ATTN)
    vst[:, BAND_PAST:BAND_PAST + tl, :] = v.astype(BF16).reshape(nb, tl, D_ATTN)

    nk_ref[...] = k.reshape(nb, tl, D_ATTN)
    nv_ref[...] = v.reshape(nb, tl, D_ATTN)

    row = lax.broadcasted_iota(jnp.int32, (tl, POOL_GRP), 0)
    for b in range(nb):
        est[b, POOL_HALO:POOL_HALO + tl, :] = u[b * tl:(b + 1) * tl, :]
        e = est[b]
        for g, w in enumerate(POOL_WINDOWS):
            eg = e[:, g * POOL_GRP:(g + 1) * POOL_GRP]
            s = eg
            sh = 1
            while sh < w:
                s = s + pltpu.roll(s, sh, 0)
                sh *= 2
            if pos0 >= POOL_PAD:
                mean = s[POOL_HALO:, :] * (1.0 / w)
            else:
                cnt = jnp.minimum(pos0 + t * tl + row + 1, w).astype(F32)
                mean = s[POOL_HALO:, :] / cnt
            dg = mean - eg[POOL_HALO:, :]
            d_sc[b * tl:(b + 1) * tl, g * POOL_GRP:(g + 1) * POOL_GRP] = dg.astype(BF16)
        tail = e[tl:tl + POOL_HALO, :]
        est[b, 0:POOL_HALO, :] = tail
        npool_ref[b] = tail

    n_grp = tl // grp
    olane = lax.broadcasted_iota(jnp.int32, (grp, LANES), 1)
    for b in range(nb):
        for g in range(n_grp):
            r0 = b * tl + g * grp
            n_early = min(kband, -(-max(0, BAND_PAST - pos0 - g * grp) // LANES) * LANES)
            if n_early:
                kcol = lax.broadcasted_iota(jnp.int32, (1, n_early), 1)
                first_pos = pos0 + t * tl + g * grp - BAND_PAST
                kmask = jnp.where(kcol + first_pos >= 0, 0.0, NEG_INF).astype(F32)
            for hp in range(N_HEADS // HEADS_PER_VREG):
                cols = slice(hp * LANES, (hp + 1) * LANES)
                qs = jnp.concatenate([qa_sc[r0:r0 + grp, cols], qb_sc[r0:r0 + grp, cols]], axis=0)
                k2 = kst[b, g * grp:g * grp + kband, cols]
                v2 = vst[b, g * grp:g * grp + kband, cols]
                s = lax.dot_general(qs, k2, (((1,), (1,)), ((), ())), preferred_element_type=F32)
                ps, ls = [], []
                for j in range(HEADS_PER_VREG):
                    sj = s[j * grp:(j + 1) * grp, :] + tab_ref[hp * HEADS_PER_VREG + j]
                    if n_early == kband:
                        sj = sj + kmask
                    elif n_early:
                        sj = jnp.concatenate([sj[:, :n_early] + kmask, sj[:, n_early:]], axis=1)
                    mj = jnp.max(sj, axis=-1, keepdims=True)
                    pj = jnp.exp2(sj - mj)
                    ls.append(jnp.sum(pj, axis=-1, keepdims=True))
                    ps.append(pj.astype(BF16))
                o = _dot(jnp.concatenate(ps, axis=0), v2)
                oa = o[0:grp, :] / ls[0]
                ob = o[grp:2 * grp, :] / ls[1]
                att_sc[r0:r0 + grp, cols] = jnp.where(olane < HEAD_DIM, oa, ob).astype(BF16)

    ys = [_dot(d_sc[:, g * POOL_GRP:(g + 1) * POOL_GRP], wgrp_ref[g]) for g in range(len(POOL_WINDOWS))]
    y = (jnp.concatenate(ys, axis=1) * pscale_ref[...]).astype(BF16)
    g_off = D_POOL + 3 * D_ATTN
    gate_p = jax.nn.sigmoid(_dot(h_sc[...], win_ref[:, g_off:g_off + D_MODEL]) + bgate_ref[:, 0:D_MODEL])
    merged = gate_p * _dot(y, wpp_ref[...])
    gate_a = jax.nn.sigmoid(_dot(h_sc[...], win_ref[:, g_off + D_MODEL:g_off + 2 * D_MODEL])
                            + bgate_ref[:, D_MODEL:2 * D_MODEL])
    merged = (merged + gate_a * _dot(att_sc[...], wap_ref[...])).astype(BF16)
    xo = x_ref[...].reshape(m, D_MODEL) + _dot(merged, wout_ref[...])
    xo_ref[...] = xo.reshape(nb, tl, D_MODEL)

    if nt > 1:
        kst[:, 0:BAND_PAST, :] = kst[:, tl:tl + BAND_PAST, :]
        vst[:, 0:BAND_PAST, :] = vst[:, tl:tl + BAND_PAST, :]


def _key_band(grp):
    return -(-(BAND_PAST + grp) // LANES) * LANES


def _band_diag(rel_bias, grp):
    span = grp + _key_band(grp) - 1
    width = -(-span // LANES) * LANES
    rel = np.clip((grp - 1) - np.arange(span) + BAND_PAST, -REL_CLIP, REL_CLIP) + REL_CLIP
    diag = rel_bias[:, :, rel].astype(F32) * LOG2E
    return jnp.pad(diag, ((0, 0), (0, 0), (0, width - span)))


def _mixer(x, cache, lw, layer, *, tl, grp, pos0):
    bsz, seq, _ = x.shape
    has_cache = cache is not None
    nb = bsz if has_cache else 1
    nt = seq // tl
    assert seq % tl == 0 and tl % grp == 0 and tl % (2 * SUBLANES) == 0
    assert not (has_cache and nt != 1)
    kband = _key_band(grp)
    srows = max(BAND_PAST + tl, (tl // grp - 1) * grp + kband)
    m = nb * tl

    row_spec = pl.BlockSpec((nb, tl, D_MODEL), lambda b, t: (b, t, 0))
    in_specs = [row_spec]
    args = [x]
    if has_cache:
        in_specs += [pl.BlockSpec((None, nb, POOL_HALO, D_POOL), lambda b, t: (layer, b, 0, 0)),
                     pl.BlockSpec((None, nb, BAND_PAST, D_ATTN), lambda b, t: (layer, b, 0, 0)),
                     pl.BlockSpec((None, nb, BAND_PAST, D_ATTN), lambda b, t: (layer, b, 0, 0))]
        args += list(cache)
    in_specs += [_layer_spec(w, layer) for w in lw]
    args += list(lw)

    out_shape = [jax.ShapeDtypeStruct((bsz, seq, D_MODEL), F32),
                 jax.ShapeDtypeStruct((bsz, POOL_HALO, D_POOL), F32),
                 jax.ShapeDtypeStruct((bsz, tl, D_ATTN), F32),
                 jax.ShapeDtypeStruct((bsz, tl, D_ATTN), F32)]
    out_specs = [row_spec,
                 pl.BlockSpec((nb, POOL_HALO, D_POOL), lambda b, t: (b, 0, 0)),
                 pl.BlockSpec((nb, tl, D_ATTN), lambda b, t: (b, 0, 0)),
                 pl.BlockSpec((nb, tl, D_ATTN), lambda b, t: (b, 0, 0))]
    scratch = [pltpu.VMEM((m, D_MODEL), BF16),
               pltpu.VMEM((nb, POOL_HALO + tl, D_POOL), F32),
               pltpu.VMEM((nb, srows, D_ATTN), BF16),
               pltpu.VMEM((nb, srows, D_ATTN), BF16),
               pltpu.VMEM((m, D_ATTN), BF16),
               pltpu.VMEM((m, D_ATTN), BF16),
               pltpu.VMEM((m, D_POOL), BF16),
               pltpu.VMEM((m, D_ATTN), BF16),
               pltpu.VMEM((N_HEADS, grp, kband), F32)]
    kern = functools.partial(_mixer_kernel, nb=nb, tl=tl, nt=nt, grp=grp, kband=kband,
                             pos0=pos0, has_cache=has_cache)
    return pl.pallas_call(
        kern,
        grid=(bsz // nb, nt),
        in_specs=in_specs,
        out_specs=out_specs,
        out_shape=out_shape,
        scratch_shapes=scratch,
        compiler_params=pltpu.CompilerParams(
            dimension_semantics=("arbitrary", "arbitrary"), vmem_limit_bytes=VMEM_LIMIT_BYTES),
        name="mixer_cached" if has_cache else "mixer",
    )(*args)


def _ffn_kernel(x_ref, p_ref, conv0_ref, gffn_ref, wup_ref, wdw_ref, bdw_ref, wdown_ref,
                gple_ref, wple_ref, wpg_ref, gfin_ref,
                xo_ref, nconv_ref, cst, h_sc, y_sc, *, nb, tl, final_norm):
    m = nb * tl
    t = pl.program_id(1)

    @pl.when(t == 0)
    def _init_state():
        cst[...] = conv0_ref[...]

    x = x_ref[...].reshape(m, D_MODEL)
    h_sc[...] = _rms(x, gffn_ref[...]).astype(BF16)
    row = lax.broadcasted_iota(jnp.int32, (tl, FF_CHUNK), 0)
    for j in range(N_FF_CHUNKS):
        cols = slice(j * FF_CHUNK, (j + 1) * FF_CHUNK)
        a_all = _dot(h_sc[...], wup_ref[:, cols])
        g_all = _dot(h_sc[...], wup_ref[:, D_FF + j * FF_CHUNK:D_FF + (j + 1) * FF_CHUNK])
        w0, w1, w2 = wdw_ref[0:1, cols], wdw_ref[1:2, cols], wdw_ref[2:3, cols]
        bias = bdw_ref[:, cols]
        for b in range(nb):
            a = a_all[b * tl:(b + 1) * tl, :]
            gate = g_all[b * tl:(b + 1) * tl, :]
            c6 = cst[b, CONV_HALO - 2:CONV_HALO - 1, cols]
            c7 = cst[b, CONV_HALO - 1:CONV_HALO, cols]
            a1 = jnp.where(row == 0, c7, pltpu.roll(a, 1, 0))
            a2 = jnp.where(row == 0, c6, jnp.where(row == 1, c7, pltpu.roll(a, 2, 0)))
            conv = bias + a2 * w0 + a1 * w1 + a * w2
            y_sc[b * tl:(b + 1) * tl, cols] = (jax.nn.gelu(conv) * gate).astype(BF16)
            tail = a[tl - CONV_HALO:tl, :]
            cst[b, :, cols] = tail
            nconv_ref[b, :, cols] = tail

    x2 = x_ref[...].reshape(m, D_MODEL) + _dot(y_sc[...], wdown_ref[...])
    hp = _rms(x2, gple_ref[...]).astype(BF16)
    gate = jax.nn.sigmoid(_dot(hp, wpg_ref[...]))
    pe = _dot(p_ref[...].reshape(m, PLE_DIM).astype(BF16), wple_ref[...])
    x3 = x2 + pe * gate
    if final_norm:
        x3 = _rms(x3, gfin_ref[...])
    xo_ref[...] = x3.reshape(nb, tl, D_MODEL)


def _ffn(x, p, conv0, lw, layer, *, tl, nb, final_norm):
    bsz, seq, _ = x.shape
    nt = seq // tl
    assert seq % tl == 0 and bsz % nb == 0 and tl % CONV_HALO == 0
    m = nb * tl
    weights = list(lw)
    row_spec = pl.BlockSpec((nb, tl, D_MODEL), lambda b, t: (b, t, 0))
    in_specs = [row_spec,
                pl.BlockSpec((None, nb, tl, PLE_DIM), lambda b, t: (layer, b, t, 0)),
                pl.BlockSpec((None, nb, CONV_HALO, D_FF), lambda b, t: (layer, b, 0, 0))]
    in_specs += [_layer_spec(w, layer) for w in weights]
    kern = functools.partial(_ffn_kernel, nb=nb, tl=tl, final_norm=final_norm)
    return pl.pallas_call(
        kern,
        grid=(bsz // nb, nt),
        in_specs=in_specs,
        out_specs=[row_spec, pl.BlockSpec((nb, CONV_HALO, D_FF), lambda b, t: (b, 0, 0))],
        out_shape=[jax.ShapeDtypeStruct((bsz, seq, D_MODEL), F32),
                   jax.ShapeDtypeStruct((bsz, CONV_HALO, D_FF), F32)],
        scratch_shapes=[pltpu.VMEM((nb, CONV_HALO, D_FF), F32),
                        pltpu.VMEM((m, D_MODEL), BF16),
                        pltpu.VMEM((m, D_FF), BF16)],
        compiler_params=pltpu.CompilerParams(
            dimension_semantics=("arbitrary", "arbitrary"), vmem_limit_bytes=VMEM_LIMIT_BYTES),
        name="ffn",
    )(x, p, conv0, *weights)


PROMPT_TILE = 512
PROMPT_GROUP = 256


def kernel(x_prompt, x_sample, cache_pool, cache_k, cache_v, cache_ffn_conv, p_prompt, p_sample,
           g_mix, w_in, b_gate, w_pool_grp, pool_scale, rel_bias, w_pool_proj, w_attn_proj, w_out,
           g_ffn, w_up, w_dw, b_dw, w_down, g_ple, w_ple, w_ple_gate, g_final):
    depth = w_in.shape[0]
    bp = x_prompt.shape[0]
    bs, ls, _ = x_sample.shape

    bf = lambda a: a.astype(BF16)
    row2 = lambda a: a[:, None, :]
    mix_w = [row2(g_mix), bf(w_in), row2(b_gate), bf(w_pool_grp), row2(pool_scale), bf(w_pool_proj),
             bf(w_attn_proj), bf(w_out)]
    ffn_w = [row2(g_ffn), bf(w_up), w_dw, row2(b_dw), bf(w_down), row2(g_ple), bf(w_ple), bf(w_ple_gate),
             jnp.broadcast_to(g_final[None, None, :], (depth, 1, D_MODEL))]
    diag_p = _band_diag(rel_bias, PROMPT_GROUP)
    diag_s = _band_diag(rel_bias, ls)

    pool0_s = jnp.pad(cache_pool, ((0, 0), (0, 0), (POOL_HALO - POOL_PAD, 0), (0, 0)))
    k0_s = bf(cache_k.reshape(depth, bs, BAND_PAST, D_ATTN))
    v0_s = bf(cache_v.reshape(depth, bs, BAND_PAST, D_ATTN))
    conv0_s = jnp.pad(cache_ffn_conv, ((0, 0), (0, 0), (CONV_HALO - cache_ffn_conv.shape[2], 0), (0, 0)))
    conv0_p = jnp.zeros((depth, bp, CONV_HALO, D_FF), F32)

    xp, xs = x_prompt, x_sample
    outs_p, outs_s = [], []
    for i in range(depth):
        fin = i == depth - 1
        xp, npool, nk, nv = _mixer(xp, None, mix_w + [diag_p], i, tl=PROMPT_TILE, grp=PROMPT_GROUP, pos0=0)
        xp, nconv = _ffn(xp, p_prompt, conv0_p, ffn_w, i, tl=PROMPT_TILE, nb=1, final_norm=fin)
        outs_p.append((npool, nk, nv, nconv))
        xs, npool, nk, nv = _mixer(xs, (pool0_s, k0_s, v0_s), mix_w + [diag_s], i, tl=ls, grp=ls, pos0=PAST_LEN)
        xs, nconv = _ffn(xs, p_sample, conv0_s, ffn_w, i, tl=ls, nb=bs, final_norm=fin)
        outs_s.append((npool, nk, nv, nconv))

    def assemble(outs, bsz):
        npool = jnp.stack([o[0][:, POOL_HALO - POOL_PAD:, :] for o in outs])
        nk = jnp.stack([o[1].reshape(bsz, -1, N_HEADS, HEAD_DIM) for o in outs])
        nv = jnp.stack([o[2].reshape(bsz, -1, N_HEADS, HEAD_DIM) for o in outs])
        nconv = jnp.stack([o[3][:, CONV_HALO - 2:, :] for o in outs])
        return npool, nk, nv, nconv

    pp, kp, vp, cp = assemble(outs_p, bp)
    ps, ks, vs, cs = assemble(outs_s, bs)
    return (xp, xs, pp, kp, vp, cp, ps, ks, vs, cs)
```

```python
import functools
import math

import jax
import jax.numpy as jnp
import numpy as np
from jax import lax
from jax.experimental import pallas as pl
from jax.experimental.pallas import tpu as pltpu

D_MODEL = 1024
CHUNK = 64
BAND_PAST = 512
D_ATTN = 512
PAST_LEN = 2048
HEAD_DIM = 64
N_HEADS = 8
REL_CLIP = 128
D_POOL = 512
POOL_WINDOWS = (2, 4, 8, 16)
POOL_GRP = 128
POOL_PAD = 15
D_FF = 2816
PLE_DIM = 256
EPS = 1e-6
NEG_INF = -1e30
LOG2E = math.log2(math.e)

LANES = 128
SUBLANES = 8
MXU_DIM = 256
VMEM_LIMIT_BYTES = 56 * 1024 * 1024

POOL_HALO = 16
CONV_HALO = SUBLANES
FF_CHUNK = MXU_DIM
N_FF_CHUNKS = D_FF // FF_CHUNK
HEADS_PER_VREG = LANES // HEAD_DIM

BF16 = jnp.bfloat16
F32 = jnp.float32


def _rms(x, g):
    return x * lax.rsqrt(jnp.mean(x * x, axis=-1, keepdims=True) + EPS) * g


def _dot(a, b):
    return jnp.dot(a, b, preferred_element_type=F32)


def _layer_spec(arr, layer):
    shape = arr.shape[1:]
    nd = len(shape)
    return pl.BlockSpec((None,) + shape, lambda b, t: (layer,) + (0,) * nd, pipeline_mode=pl.Buffered(1))


def _mixer_kernel(*refs, nb, tl, nt, grp, kband, pos0, has_cache):
    it = iter(refs)
    x_ref = next(it)
    if has_cache:
        pool0_ref, k0_ref, v0_ref = next(it), next(it), next(it)
    (gmix_ref, win_ref, bgate_ref, wgrp_ref, pscale_ref, wpp_ref, wap_ref, wout_ref, diag_ref,
     xo_ref, npool_ref, nk_ref, nv_ref,
     h_sc, est, kst, vst, qa_sc, qb_sc, d_sc, att_sc, tab_ref) = it

    m = nb * tl
    t = pl.program_id(1)
    srows = kst.shape[1]

    @pl.when((pl.program_id(0) == 0) & (t == 0))
    def _build_band_table():
        width = diag_ref.shape[1]
        i = lax.broadcasted_iota(jnp.int32, (grp, kband), 0)
        j = lax.broadcasted_iota(jnp.int32, (grp, kband), 1)
        lo = jnp.bitwise_and(i, -CHUNK)
        valid = (j >= lo) & (j < lo + BAND_PAST + CHUNK) & (j < BAND_PAST + grp)
        for hd in range(N_HEADS):
            d = jnp.broadcast_to(diag_ref[hd:hd + 1, :], (grp, width))
            shifted = pltpu.roll(d, width - (grp - 1), 1, stride=1, stride_axis=0)
            tab_ref[hd] = jnp.where(valid, shifted[:, 0:kband], NEG_INF)

    @pl.when(t == 0)
    def _init_state():
        if has_cache:
            est[:, 0:POOL_HALO, :] = pool0_ref[...]
            kst[:, 0:BAND_PAST, :] = k0_ref[...]
            vst[:, 0:BAND_PAST, :] = v0_ref[...]
        else:
            est[:, 0:POOL_HALO, :] = jnp.zeros((nb, POOL_HALO, D_POOL), F32)
            kst[:, 0:BAND_PAST, :] = jnp.zeros((nb, BAND_PAST, D_ATTN), BF16)
            vst[:, 0:BAND_PAST, :] = jnp.zeros((nb, BAND_PAST, D_ATTN), BF16)
        if srows > BAND_PAST + tl:
            pad = srows - BAND_PAST - tl
            kst[:, BAND_PAST + tl:, :] = jnp.zeros((nb, pad, D_ATTN), BF16)
            vst[:, BAND_PAST + tl:, :] = jnp.zeros((nb, pad, D_ATTN), BF16)

    x = x_ref[...].reshape(m, D_MODEL)
    h = _rms(x, gmix_ref[...]).astype(BF16)
    h_sc[...] = h
    u = _dot(h, win_ref[:, 0:D_POOL])
    q = _dot(h, win_ref[:, D_POOL:D_POOL + D_ATTN]) * (HEAD_DIM ** -0.5 * LOG2E)
    k = _dot(h, win_ref[:, D_POOL + D_ATTN:D_POOL + 2 * D_ATTN])
    v = _dot(h, win_ref[:, D_POOL + 2 * D_ATTN:D_POOL + 3 * D_ATTN])

    lane = lax.broadcasted_iota(jnp.int32, (m, D_ATTN), 1)
    first_head = (lane % LANES) < HEAD_DIM
    qa_sc[...] = jnp.where(first_head, q, 0.0).astype(BF16)
    qb_sc[...] = jnp.where(first_head, 0.0, q).astype(BF16)

    kst[:, BAND_PAST:BAND_PAST + tl, :] = k.astype(BF16).reshape(nb, tl, D_ATTN)
    vst[:, BAND_PAST:BAND_PAST + tl, :] = v.astype(BF16).reshape(nb, tl, D_ATTN)

    nk_ref[...] = k.reshape(nb, tl, D_ATTN)
    nv_ref[...] = v.reshape(nb, tl, D_ATTN)

    row = lax.broadcasted_iota(jnp.int32, (tl, POOL_GRP), 0)
    for b in range(nb):
        est[b, POOL_HALO:POOL_HALO + tl, :] = u[b * tl:(b + 1) * tl, :]
        e = est[b]
        for g, w in enumerate(POOL_WINDOWS):
            eg = e[:, g * POOL_GRP:(g + 1) * POOL_GRP]
            s = eg
            sh = 1
            while sh < w:
                s = s + pltpu.roll(s, sh, 0)
                sh *= 2
            if pos0 >= POOL_PAD:
                mean = s[POOL_HALO:, :] * (1.0 / w)
            else:
                cnt = jnp.minimum(pos0 + t * tl + row + 1, w).astype(F32)
                mean = s[POOL_HALO:, :] / cnt
            dg = mean - eg[POOL_HALO:, :]
            d_sc[b * tl:(b + 1) * tl, g * POOL_GRP:(g + 1) * POOL_GRP] = dg.astype(BF16)
        tail = e[tl:tl + POOL_HALO, :]
        est[b, 0:POOL_HALO, :] = tail
        npool_ref[b] = tail

    n_grp = tl // grp
    olane = lax.broadcasted_iota(jnp.int32, (grp, LANES), 1)
    for g in range(n_grp):
        n_early = min(kband, -(-max(0, BAND_PAST - pos0 - g * grp) // LANES) * LANES)
        if n_early:
            kcol = lax.broadcasted_iota(jnp.int32, (1, n_early), 1)
            first_pos = pos0 + t * tl + g * grp - BAND_PAST
            kmask = jnp.where(kcol + first_pos >= 0, 0.0, NEG_INF).astype(F32)
        for hp in range(N_HEADS // HEADS_PER_VREG):
            cols = slice(hp * LANES, (hp + 1) * LANES)
            ss = []
            for b in range(nb):
                r0 = b * tl + g * grp
                qs = jnp.concatenate([qa_sc[r0:r0 + grp, cols], qb_sc[r0:r0 + grp, cols]], axis=0)
                k2 = kst[b, g * grp:g * grp + kband, cols]
                ss.append(lax.dot_general(qs, k2, (((1,), (1,)), ((), ())), preferred_element_type=F32))
            tab2 = [tab_ref[hp * HEADS_PER_VREG + j] for j in range(HEADS_PER_VREG)]
            s = jnp.concatenate(ss, axis=0) + jnp.concatenate(tab2 * nb, axis=0)
            if n_early == kband:
                s = s + kmask
            elif n_early:
                s = jnp.concatenate([s[:, :n_early] + kmask, s[:, n_early:]], axis=1)
            p = jnp.exp2(s - jnp.max(s, axis=-1, keepdims=True))
            inv = 1.0 / jnp.sum(p, axis=-1, keepdims=True)
            p = p.astype(BF16)
            for b in range(nb):
                r0 = b * tl + g * grp
                rows = slice(b * 2 * grp, (b + 1) * 2 * grp)
                v2 = vst[b, g * grp:g * grp + kband, cols]
                o = _dot(p[rows, :], v2) * inv[rows, :]
                att_sc[r0:r0 + grp, cols] = jnp.where(olane < HEAD_DIM, o[0:grp, :], o[grp:, :]).astype(BF16)

    ys = [_dot(d_sc[:, g * POOL_GRP:(g + 1) * POOL_GRP], wgrp_ref[g]) for g in range(len(POOL_WINDOWS))]
    y = (jnp.concatenate(ys, axis=1) * pscale_ref[...]).astype(BF16)
    g_off = D_POOL + 3 * D_ATTN
    gate_p = jax.nn.sigmoid(_dot(h_sc[...], win_ref[:, g_off:g_off + D_MODEL]) + bgate_ref[:, 0:D_MODEL])
    merged = gate_p * _dot(y, wpp_ref[...])
    gate_a = jax.nn.sigmoid(_dot(h_sc[...], win_ref[:, g_off + D_MODEL:g_off + 2 * D_MODEL])
                            + bgate_ref[:, D_MODEL:2 * D_MODEL])
    merged = (merged + gate_a * _dot(att_sc[...], wap_ref[...])).astype(BF16)
    xo = x_ref[...].reshape(m, D_MODEL) + _dot(merged, wout_ref[...])
    xo_ref[...] = xo.reshape(nb, tl, D_MODEL)

    if nt > 1:
        kst[:, 0:BAND_PAST, :] = kst[:, tl:tl + BAND_PAST, :]
        vst[:, 0:BAND_PAST, :] = vst[:, tl:tl + BAND_PAST, :]


def _key_band(grp):
    return -(-(BAND_PAST + grp) // LANES) * LANES


def _band_diag(rel_bias, grp):
    span = grp + _key_band(grp) - 1
    width = -(-span // LANES) * LANES
    rel = np.clip((grp - 1) - np.arange(span) + BAND_PAST, -REL_CLIP, REL_CLIP) + REL_CLIP
    diag = rel_bias[:, :, rel].astype(F32) * LOG2E
    return jnp.pad(diag, ((0, 0), (0, 0), (0, width - span)))


def _mixer(x, cache, lw, layer, *, tl, grp, pos0):
    bsz, seq, _ = x.shape
    has_cache = cache is not None
    nb = bsz if has_cache else 1
    nt = seq // tl
    assert seq % tl == 0 and tl % grp == 0 and tl % (2 * SUBLANES) == 0
    assert not (has_cache and nt != 1)
    kband = _key_band(grp)
    srows = max(BAND_PAST + tl, (tl // grp - 1) * grp + kband)
    m = nb * tl

    row_spec = pl.BlockSpec((nb, tl, D_MODEL), lambda b, t: (b, t, 0))
    in_specs = [row_spec]
    args = [x]
    if has_cache:
        in_specs += [pl.BlockSpec((None, nb, POOL_HALO, D_POOL), lambda b, t: (layer, b, 0, 0)),
                     pl.BlockSpec((None, nb, BAND_PAST, D_ATTN), lambda b, t: (layer, b, 0, 0)),
                     pl.BlockSpec((None, nb, BAND_PAST, D_ATTN), lambda b, t: (layer, b, 0, 0))]
        args += list(cache)
    in_specs += [_layer_spec(w, layer) for w in lw]
    args += list(lw)

    out_shape = [jax.ShapeDtypeStruct((bsz, seq, D_MODEL), F32),
                 jax.ShapeDtypeStruct((bsz, POOL_HALO, D_POOL), F32),
                 jax.ShapeDtypeStruct((bsz, tl, D_ATTN), F32),
                 jax.ShapeDtypeStruct((bsz, tl, D_ATTN), F32)]
    out_specs = [row_spec,
                 pl.BlockSpec((nb, POOL_HALO, D_POOL), lambda b, t: (b, 0, 0)),
                 pl.BlockSpec((nb, tl, D_ATTN), lambda b, t: (b, 0, 0)),
                 pl.BlockSpec((nb, tl, D_ATTN), lambda b, t: (b, 0, 0))]
    scratch = [pltpu.VMEM((m, D_MODEL), BF16),
               pltpu.VMEM((nb, POOL_HALO + tl, D_POOL), F32),
               pltpu.VMEM((nb, srows, D_ATTN), BF16),
               pltpu.VMEM((nb, srows, D_ATTN), BF16),
               pltpu.VMEM((m, D_ATTN), BF16),
               pltpu.VMEM((m, D_ATTN), BF16),
               pltpu.VMEM((m, D_POOL), BF16),
               pltpu.VMEM((m, D_ATTN), BF16),
               pltpu.VMEM((N_HEADS, grp, kband), F32)]
    kern = functools.partial(_mixer_kernel, nb=nb, tl=tl, nt=nt, grp=grp, kband=kband,
                             pos0=pos0, has_cache=has_cache)
    return pl.pallas_call(
        kern,
        grid=(bsz // nb, nt),
        in_specs=in_specs,
        out_specs=out_specs,
        out_shape=out_shape,
        scratch_shapes=scratch,
        compiler_params=pltpu.CompilerParams(
            dimension_semantics=("arbitrary", "arbitrary"), vmem_limit_bytes=VMEM_LIMIT_BYTES),
        name="mixer_cached" if has_cache else "mixer",
    )(*args)


def _ffn_kernel(x_ref, p_ref, conv0_ref, gffn_ref, wup_ref, wdw_ref, bdw_ref, wdown_ref,
                gple_ref, wple_ref, wpg_ref, gfin_ref,
                xo_ref, nconv_ref, cst, h_sc, y_sc, *, nb, tl, final_norm):
    m = nb * tl
    t = pl.program_id(1)

    @pl.when(t == 0)
    def _init_state():
        cst[...] = conv0_ref[...]

    x = x_ref[...].reshape(m, D_MODEL)
    h_sc[...] = _rms(x, gffn_ref[...]).astype(BF16)
    row = lax.broadcasted_iota(jnp.int32, (tl, FF_CHUNK), 0)
    for j in range(N_FF_CHUNKS):
        cols = slice(j * FF_CHUNK, (j + 1) * FF_CHUNK)
        a_all = _dot(h_sc[...], wup_ref[:, cols])
        g_all = _dot(h_sc[...], wup_ref[:, D_FF + j * FF_CHUNK:D_FF + (j + 1) * FF_CHUNK])
        w0, w1, w2 = wdw_ref[0:1, cols], wdw_ref[1:2, cols], wdw_ref[2:3, cols]
        bias = bdw_ref[:, cols]
        for b in range(nb):
            a = a_all[b * tl:(b + 1) * tl, :]
            gate = g_all[b * tl:(b + 1) * tl, :]
            c6 = cst[b, CONV_HALO - 2:CONV_HALO - 1, cols]
            c7 = cst[b, CONV_HALO - 1:CONV_HALO, cols]
            a1 = jnp.where(row == 0, c7, pltpu.roll(a, 1, 0))
            a2 = jnp.where(row == 0, c6, jnp.where(row == 1, c7, pltpu.roll(a, 2, 0)))
            conv = bias + a2 * w0 + a1 * w1 + a * w2
            y_sc[b * tl:(b + 1) * tl, cols] = (jax.nn.gelu(conv) * gate).astype(BF16)
            tail = a[tl - CONV_HALO:tl, :]
            cst[b, :, cols] = tail
            nconv_ref[b, :, cols] = tail

    x2 = x_ref[...].reshape(m, D_MODEL) + _dot(y_sc[...], wdown_ref[...])
    hp = _rms(x2, gple_ref[...]).astype(BF16)
    gate = jax.nn.sigmoid(_dot(hp, wpg_ref[...]))
    pe = _dot(p_ref[...].reshape(m, PLE_DIM).astype(BF16), wple_ref[...])
    x3 = x2 + pe * gate
    if final_norm:
        x3 = _rms(x3, gfin_ref[...])
    xo_ref[...] = x3.reshape(nb, tl, D_MODEL)


def _ffn(x, p, conv0, lw, layer, *, tl, nb, final_norm):
    bsz, seq, _ = x.shape
    nt = seq // tl
    assert seq % tl == 0 and bsz % nb == 0 and tl % CONV_HALO == 0
    m = nb * tl
    weights = list(lw)
    row_spec = pl.BlockSpec((nb, tl, D_MODEL), lambda b, t: (b, t, 0))
    in_specs = [row_spec,
                pl.BlockSpec((None, nb, tl, PLE_DIM), lambda b, t: (layer, b, t, 0)),
                pl.BlockSpec((None, nb, CONV_HALO, D_FF), lambda b, t: (layer, b, 0, 0))]
    in_specs += [_layer_spec(w, layer) for w in weights]
    kern = functools.partial(_ffn_kernel, nb=nb, tl=tl, final_norm=final_norm)
    return pl.pallas_call(
        kern,
        grid=(bsz // nb, nt),
        in_specs=in_specs,
        out_specs=[row_spec, pl.BlockSpec((nb, CONV_HALO, D_FF), lambda b, t: (b, 0, 0))],
        out_shape=[jax.ShapeDtypeStruct((bsz, seq, D_MODEL), F32),
                   jax.ShapeDtypeStruct((bsz, CONV_HALO, D_FF), F32)],
        scratch_shapes=[pltpu.VMEM((nb, CONV_HALO, D_FF), F32),
                        pltpu.VMEM((m, D_MODEL), BF16),
                        pltpu.VMEM((m, D_FF), BF16)],
        compiler_params=pltpu.CompilerParams(
            dimension_semantics=("arbitrary", "arbitrary"), vmem_limit_bytes=VMEM_LIMIT_BYTES),
        name="ffn",
    )(x, p, conv0, *weights)


PROMPT_TILE = 512
PROMPT_FFN_TILE = 1024
PROMPT_GROUP = 256


def kernel(x_prompt, x_sample, cache_pool, cache_k, cache_v, cache_ffn_conv, p_prompt, p_sample,
           g_mix, w_in, b_gate, w_pool_grp, pool_scale, rel_bias, w_pool_proj, w_attn_proj, w_out,
           g_ffn, w_up, w_dw, b_dw, w_down, g_ple, w_ple, w_ple_gate, g_final):
    depth = w_in.shape[0]
    bp = x_prompt.shape[0]
    bs, ls, _ = x_sample.shape

    bf = lambda a: a.astype(BF16)
    row2 = lambda a: a[:, None, :]
    mix_w = [row2(g_mix), bf(w_in), row2(b_gate), bf(w_pool_grp), row2(pool_scale), bf(w_pool_proj),
             bf(w_attn_proj), bf(w_out)]
    ffn_w = [row2(g_ffn), bf(w_up), w_dw, row2(b_dw), bf(w_down), row2(g_ple), bf(w_ple), bf(w_ple_gate),
             jnp.broadcast_to(g_final[None, None, :], (depth, 1, D_MODEL))]
    diag_p = _band_diag(rel_bias, PROMPT_GROUP)
    diag_s = _band_diag(rel_bias, ls)

    pool0_s = jnp.pad(cache_pool, ((0, 0), (0, 0), (POOL_HALO - POOL_PAD, 0), (0, 0)))
    k0_s = bf(cache_k.reshape(depth, bs, BAND_PAST, D_ATTN))
    v0_s = bf(cache_v.reshape(depth, bs, BAND_PAST, D_ATTN))
    conv0_s = jnp.pad(cache_ffn_conv, ((0, 0), (0, 0), (CONV_HALO - cache_ffn_conv.shape[2], 0), (0, 0)))
    conv0_p = jnp.zeros((depth, bp, CONV_HALO, D_FF), F32)

    xp, xs = x_prompt, x_sample
    outs_p, outs_s = [], []
    for i in range(depth):
        fin = i == depth - 1
        xp, npool, nk, nv = _mixer(xp, None, mix_w + [diag_p], i, tl=PROMPT_TILE, grp=PROMPT_GROUP, pos0=0)
        xp, nconv = _ffn(xp, p_prompt, conv0_p, ffn_w, i, tl=PROMPT_FFN_TILE, nb=1, final_norm=fin)
        outs_p.append((npool, nk, nv, nconv))
        xs, npool, nk, nv = _mixer(xs, (pool0_s, k0_s, v0_s), mix_w + [diag_s], i, tl=ls, grp=ls, pos0=PAST_LEN)
        xs, nconv = _ffn(xs, p_sample, conv0_s, ffn_w, i, tl=ls, nb=bs, final_norm=fin)
        outs_s.append((npool, nk, nv, nconv))

    def assemble(outs, bsz):
        npool = jnp.stack([o[0][:, POOL_HALO - POOL_PAD:, :] for o in outs])
        nk = jnp.stack([o[1].reshape(bsz, -1, N_HEADS, HEAD_DIM) for o in outs])
        nv = jnp.stack([o[2].reshape(bsz, -1, N_HEADS, HEAD_DIM) for o in outs])
        nconv = jnp.stack([o[3][:, CONV_HALO - 2:, :] for o in outs])
        return npool, nk, nv, nconv

    pp, kp, vp, cp = assemble(outs_p, bp)
    ps, ks, vs, cs = assemble(outs_s, bs)
    return (xp, xs, pp, kp, vp, cp, ps, ks, vs, cs)
```

```python
import functools
import math

import jax
import jax.numpy as jnp
import numpy as np
from jax import lax
from jax.experimental import pallas as pl
from jax.experimental.pallas import tpu as pltpu

D_MODEL = 1024
CHUNK = 64
BAND_PAST = 512
D_ATTN = 512
PAST_LEN = 2048
HEAD_DIM = 64
N_HEADS = 8
REL_CLIP = 128
D_POOL = 512
POOL_WINDOWS = (2, 4, 8, 16)
POOL_GRP = 128
POOL_PAD = 15
D_FF = 2816
PLE_DIM = 256
EPS = 1e-6
NEG_INF = -1e30
LOG2E = math.log2(math.e)

LANES = 128
SUBLANES = 8
MXU_DIM = 256
VMEM_LIMIT_BYTES = 56 * 1024 * 1024

POOL_HALO = 16
CONV_HALO = SUBLANES
FF_CHUNK = MXU_DIM
N_FF_CHUNKS = D_FF // FF_CHUNK
HEADS_PER_VREG = LANES // HEAD_DIM

BF16 = jnp.bfloat16
F32 = jnp.float32


def _rms(x, g):
    return x * lax.rsqrt(jnp.mean(x * x, axis=-1, keepdims=True) + EPS) * g


def _dot(a, b):
    return jnp.dot(a, b, preferred_element_type=F32)


def _weight_spec(w):
    arr, layer = w
    if layer is None:
        block, lead = arr.shape, ()
    else:
        block, lead = (None,) + arr.shape[1:], (layer,)
    nd = arr.ndim - len(lead)
    return pl.BlockSpec(block, lambda b, t: lead + (0,) * nd, pipeline_mode=pl.Buffered(1))


def _cast_plan(w, layer, n_steps, nt):
    _, rows, cols = w.shape
    rb = next(r for r in range(2 * SUBLANES, rows + 1, 2 * SUBLANES) if rows % r == 0 and rows // r <= n_steps)
    last = rows // rb - 1
    return (pl.BlockSpec((None, rb, cols), lambda b, t: (layer, jnp.minimum(b * nt + t, last), 0)),
            pl.BlockSpec((rb, cols), lambda b, t: (jnp.minimum(b * nt + t, last), 0)),
            jax.ShapeDtypeStruct((rows, cols), BF16))


def _mixer_kernel(*refs, nb, tl, nt, grp, kband, pos0, has_cache, n_cast):
    it = iter(refs)
    x_ref = next(it)
    if has_cache:
        pool0_ref, k0_ref, v0_ref = next(it), next(it), next(it)
    gmix_ref, win_ref, bgate_ref, wgrp_ref, pscale_ref, wpp_ref, wap_ref, wout_ref, diag_ref = (
        next(it) for _ in range(9))
    cast_in = [next(it) for _ in range(n_cast)]
    xo_ref, npool_ref, nk_ref, nv_ref = (next(it) for _ in range(4))
    cast_out = [next(it) for _ in range(n_cast)]
    h_sc, est, kst, vst, qa_sc, qb_sc, d_sc, att_sc, tab_ref = it

    m = nb * tl
    t = pl.program_id(1)
    srows = kst.shape[1]

    @pl.when((pl.program_id(0) == 0) & (t == 0))
    def _build_band_table():
        width = diag_ref.shape[1]
        i = lax.broadcasted_iota(jnp.int32, (grp, kband), 0)
        j = lax.broadcasted_iota(jnp.int32, (grp, kband), 1)
        lo = jnp.bitwise_and(i, -CHUNK)
        valid = (j >= lo) & (j < lo + BAND_PAST + CHUNK) & (j < BAND_PAST + grp)
        for hd in range(N_HEADS):
            d = jnp.broadcast_to(diag_ref[hd:hd + 1, :], (grp, width))
            shifted = pltpu.roll(d, width - (grp - 1), 1, stride=1, stride_axis=0)
            tab_ref[hd] = jnp.where(valid, shifted[:, 0:kband], NEG_INF)

    @pl.when(t == 0)
    def _init_state():
        if has_cache:
            est[:, 0:POOL_HALO, :] = pool0_ref[...]
            kst[:, 0:BAND_PAST, :] = k0_ref[...]
            vst[:, 0:BAND_PAST, :] = v0_ref[...]
        else:
            est[:, 0:POOL_HALO, :] = jnp.zeros((nb, POOL_HALO, D_POOL), F32)
            kst[:, 0:BAND_PAST, :] = jnp.zeros((nb, BAND_PAST, D_ATTN), BF16)
            vst[:, 0:BAND_PAST, :] = jnp.zeros((nb, BAND_PAST, D_ATTN), BF16)
        if srows > BAND_PAST + tl:
            pad = srows - BAND_PAST - tl
            kst[:, BAND_PAST + tl:, :] = jnp.zeros((nb, pad, D_ATTN), BF16)
            vst[:, BAND_PAST + tl:, :] = jnp.zeros((nb, pad, D_ATTN), BF16)

    for src, dst in zip(cast_in, cast_out):
        dst[...] = src[...].astype(BF16)

    x = x_ref[...].reshape(m, D_MODEL)
    h = _rms(x, gmix_ref[...]).astype(BF16)
    h_sc[...] = h
    u = _dot(h, win_ref[:, 0:D_POOL])
    q = _dot(h, win_ref[:, D_POOL:D_POOL + D_ATTN]) * (HEAD_DIM ** -0.5 * LOG2E)
    k = _dot(h, win_ref[:, D_POOL + D_ATTN:D_POOL + 2 * D_ATTN])
    v = _dot(h, win_ref[:, D_POOL + 2 * D_ATTN:D_POOL + 3 * D_ATTN])

    lane = lax.broadcasted_iota(jnp.int32, (m, D_ATTN), 1)
    first_head = (lane % LANES) < HEAD_DIM
    qa_sc[...] = jnp.where(first_head, q, 0.0).astype(BF16)
    qb_sc[...] = jnp.where(first_head, 0.0, q).astype(BF16)

    kst[:, BAND_PAST:BAND_PAST + tl, :] = k.astype(BF16).reshape(nb, tl, D_ATTN)
    vst[:, BAND_PAST:BAND_PAST + tl, :] = v.astype(BF16).reshape(nb, tl, D_ATTN)

    nk_ref[...] = k.reshape(nb, tl, D_ATTN)
    nv_ref[...] = v.reshape(nb, tl, D_ATTN)

    row = lax.broadcasted_iota(jnp.int32, (tl, POOL_GRP), 0)
    for b in range(nb):
        est[b, POOL_HALO:POOL_HALO + tl, :] = u[b * tl:(b + 1) * tl, :]
        e = est[b]
        for g, w in enumerate(POOL_WINDOWS):
            eg = e[:, g * POOL_GRP:(g + 1) * POOL_GRP]
            s = eg
            sh = 1
            while sh < w:
                s = s + pltpu.roll(s, sh, 0)
                sh *= 2
            if pos0 >= POOL_PAD:
                mean = s[POOL_HALO:, :] * (1.0 / w)
            else:
                cnt = jnp.minimum(pos0 + t * tl + row + 1, w).astype(F32)
                mean = s[POOL_HALO:, :] / cnt
            dg = mean - eg[POOL_HALO:, :]
            d_sc[b * tl:(b + 1) * tl, g * POOL_GRP:(g + 1) * POOL_GRP] = dg.astype(BF16)
        tail = e[tl:tl + POOL_HALO, :]
        est[b, 0:POOL_HALO, :] = tail
        npool_ref[b] = tail

    n_grp = tl // grp
    olane = lax.broadcasted_iota(jnp.int32, (grp, LANES), 1)
    for g in range(n_grp):
        n_early = min(kband, -(-max(0, BAND_PAST - pos0 - g * grp) // LANES) * LANES)
        if n_early:
            kcol = lax.broadcasted_iota(jnp.int32, (1, n_early), 1)
            first_pos = pos0 + t * tl + g * grp - BAND_PAST
            kmask = jnp.where(kcol + first_pos >= 0, 0.0, NEG_INF).astype(F32)
        for hp in range(N_HEADS // HEADS_PER_VREG):
            cols = slice(hp * LANES, (hp + 1) * LANES)
            ss = []
            for b in range(nb):
                r0 = b * tl + g * grp
                qs = jnp.concatenate([qa_sc[r0:r0 + grp, cols], qb_sc[r0:r0 + grp, cols]], axis=0)
                k2 = kst[b, g * grp:g * grp + kband, cols]
                ss.append(lax.dot_general(qs, k2, (((1,), (1,)), ((), ())), preferred_element_type=F32))
            tab2 = [tab_ref[hp * HEADS_PER_VREG + j] for j in range(HEADS_PER_VREG)]
            s = jnp.concatenate(ss, axis=0) + jnp.concatenate(tab2 * nb, axis=0)
            if n_early == kband:
                s = s + kmask
            elif n_early:
                s = jnp.concatenate([s[:, :n_early] + kmask, s[:, n_early:]], axis=1)
            p = jnp.exp2(s - jnp.max(s, axis=-1, keepdims=True))
            inv = 1.0 / jnp.sum(p, axis=-1, keepdims=True)
            p = p.astype(BF16)
            for b in range(nb):
                r0 = b * tl + g * grp
                rows = slice(b * 2 * grp, (b + 1) * 2 * grp)
                v2 = vst[b, g * grp:g * grp + kband, cols]
                o = _dot(p[rows, :], v2) * inv[rows, :]
                att_sc[r0:r0 + grp, cols] = jnp.where(olane < HEAD_DIM, o[0:grp, :], o[grp:, :]).astype(BF16)

    ys = [_dot(d_sc[:, g * POOL_GRP:(g + 1) * POOL_GRP], wgrp_ref[g]) for g in range(len(POOL_WINDOWS))]
    y = (jnp.concatenate(ys, axis=1) * pscale_ref[...]).astype(BF16)
    g_off = D_POOL + 3 * D_ATTN
    gate_p = jax.nn.sigmoid(_dot(h_sc[...], win_ref[:, g_off:g_off + D_MODEL]) + bgate_ref[:, 0:D_MODEL])
    merged = gate_p * _dot(y, wpp_ref[...])
    gate_a = jax.nn.sigmoid(_dot(h_sc[...], win_ref[:, g_off + D_MODEL:g_off + 2 * D_MODEL])
                            + bgate_ref[:, D_MODEL:2 * D_MODEL])
    merged = (merged + gate_a * _dot(att_sc[...], wap_ref[...])).astype(BF16)
    xo = x_ref[...].reshape(m, D_MODEL) + _dot(merged, wout_ref[...])
    xo_ref[...] = xo.reshape(nb, tl, D_MODEL)

    if nt > 1:
        kst[:, 0:BAND_PAST, :] = kst[:, tl:tl + BAND_PAST, :]
        vst[:, 0:BAND_PAST, :] = vst[:, tl:tl + BAND_PAST, :]


def _key_band(grp):
    return -(-(BAND_PAST + grp) // LANES) * LANES


def _band_diag(rel_bias, grp):
    span = grp + _key_band(grp) - 1
    width = -(-span // LANES) * LANES
    rel = np.clip((grp - 1) - np.arange(span) + BAND_PAST, -REL_CLIP, REL_CLIP) + REL_CLIP
    diag = rel_bias[:, :, rel].astype(F32) * LOG2E
    return jnp.pad(diag, ((0, 0), (0, 0), (0, width - span)))


def _mixer(x, cache, lw, layer, cast, *, tl, grp, pos0):
    bsz, seq, _ = x.shape
    has_cache = cache is not None
    nb = bsz if has_cache else 1
    nt = seq // tl
    assert seq % tl == 0 and tl % grp == 0 and tl % (2 * SUBLANES) == 0
    assert not (has_cache and nt != 1)
    kband = _key_band(grp)
    srows = max(BAND_PAST + tl, (tl // grp - 1) * grp + kband)
    m = nb * tl

    row_spec = pl.BlockSpec((nb, tl, D_MODEL), lambda b, t: (b, t, 0))
    in_specs = [row_spec]
    args = [x]
    if has_cache:
        in_specs += [pl.BlockSpec((None, nb, POOL_HALO, D_POOL), lambda b, t: (layer, b, 0, 0)),
                     pl.BlockSpec((None, nb, BAND_PAST, D_ATTN), lambda b, t: (layer, b, 0, 0)),
                     pl.BlockSpec((None, nb, BAND_PAST, D_ATTN), lambda b, t: (layer, b, 0, 0))]
        args += list(cache)
    in_specs += [_weight_spec(w) for w in lw]
    args += [w[0] for w in lw]
    plans = [_cast_plan(w, layer, (bsz // nb) * nt, nt) for w in cast]
    in_specs += [plan[0] for plan in plans]
    args += list(cast)

    out_shape = [jax.ShapeDtypeStruct((bsz, seq, D_MODEL), F32),
                 jax.ShapeDtypeStruct((bsz, POOL_HALO, D_POOL), F32),
                 jax.ShapeDtypeStruct((bsz, tl, D_ATTN), F32),
                 jax.ShapeDtypeStruct((bsz, tl, D_ATTN), F32)] + [plan[2] for plan in plans]
    out_specs = [row_spec,
                 pl.BlockSpec((nb, POOL_HALO, D_POOL), lambda b, t: (b, 0, 0)),
                 pl.BlockSpec((nb, tl, D_ATTN), lambda b, t: (b, 0, 0)),
                 pl.BlockSpec((nb, tl, D_ATTN), lambda b, t: (b, 0, 0))] + [plan[1] for plan in plans]
    scratch = [pltpu.VMEM((m, D_MODEL), BF16),
               pltpu.VMEM((nb, POOL_HALO + tl, D_POOL), F32),
               pltpu.VMEM((nb, srows, D_ATTN), BF16),
               pltpu.VMEM((nb, srows, D_ATTN), BF16),
               pltpu.VMEM((m, D_ATTN), BF16),
               pltpu.VMEM((m, D_ATTN), BF16),
               pltpu.VMEM((m, D_POOL), BF16),
               pltpu.VMEM((m, D_ATTN), BF16),
               pltpu.VMEM((N_HEADS, grp, kband), F32)]
    kern = functools.partial(_mixer_kernel, nb=nb, tl=tl, nt=nt, grp=grp, kband=kband,
                             pos0=pos0, has_cache=has_cache, n_cast=len(cast))
    return pl.pallas_call(
        kern,
        grid=(bsz // nb, nt),
        in_specs=in_specs,
        out_specs=out_specs,
        out_shape=out_shape,
        scratch_shapes=scratch,
        compiler_params=pltpu.CompilerParams(
            dimension_semantics=("arbitrary", "arbitrary"), vmem_limit_bytes=VMEM_LIMIT_BYTES),
        name="mixer_cached" if has_cache else "mixer",
    )(*args)


def _ffn_kernel(x_ref, p_ref, conv0_ref, gffn_ref, wup_ref, wdw_ref, bdw_ref, wdown_ref,
                gple_ref, wple_ref, wpg_ref, gfin_ref, *rest, nb, tl, n_cast, final_norm):
    cast_in = rest[:n_cast]
    xo_ref, nconv_ref = rest[n_cast:n_cast + 2]
    cast_out = rest[n_cast + 2:2 * n_cast + 2]
    cst, h_sc, y_sc = rest[2 * n_cast + 2:]
    m = nb * tl
    t = pl.program_id(1)

    @pl.when(t == 0)
    def _init_state():
        cst[...] = conv0_ref[...]

    for src, dst in zip(cast_in, cast_out):
        dst[...] = src[...].astype(BF16)

    x = x_ref[...].reshape(m, D_MODEL)
    h_sc[...] = _rms(x, gffn_ref[...]).astype(BF16)
    row = lax.broadcasted_iota(jnp.int32, (tl, FF_CHUNK), 0)
    for j in range(N_FF_CHUNKS):
        cols = slice(j * FF_CHUNK, (j + 1) * FF_CHUNK)
        a_all = _dot(h_sc[...], wup_ref[:, cols])
        g_all = _dot(h_sc[...], wup_ref[:, D_FF + j * FF_CHUNK:D_FF + (j + 1) * FF_CHUNK])
        w0, w1, w2 = wdw_ref[0:1, cols], wdw_ref[1:2, cols], wdw_ref[2:3, cols]
        bias = bdw_ref[:, cols]
        for b in range(nb):
            a = a_all[b * tl:(b + 1) * tl, :]
            gate = g_all[b * tl:(b + 1) * tl, :]
            c6 = cst[b, CONV_HALO - 2:CONV_HALO - 1, cols]
            c7 = cst[b, CONV_HALO - 1:CONV_HALO, cols]
            a1 = jnp.where(row == 0, c7, pltpu.roll(a, 1, 0))
            a2 = jnp.where(row == 0, c6, jnp.where(row == 1, c7, pltpu.roll(a, 2, 0)))
            conv = bias + a2 * w0 + a1 * w1 + a * w2
            y_sc[b * tl:(b + 1) * tl, cols] = (jax.nn.gelu(conv) * gate).astype(BF16)
            tail = a[tl - CONV_HALO:tl, :]
            cst[b, :, cols] = tail
            nconv_ref[b, :, cols] = tail

    x2 = x_ref[...].reshape(m, D_MODEL) + _dot(y_sc[...], wdown_ref[...])
    hp = _rms(x2, gple_ref[...]).astype(BF16)
    gate = jax.nn.sigmoid(_dot(hp, wpg_ref[...]))
    pe = _dot(p_ref[...].reshape(m, PLE_DIM).astype(BF16), wple_ref[...])
    x3 = x2 + pe * gate
    if final_norm:
        x3 = _rms(x3, gfin_ref[...])
    xo_ref[...] = x3.reshape(nb, tl, D_MODEL)


def _ffn(x, p, conv0, lw, layer, cast, *, tl, nb, final_norm):
    bsz, seq, _ = x.shape
    nt = seq // tl
    assert seq % tl == 0 and bsz % nb == 0 and tl % CONV_HALO == 0
    m = nb * tl
    row_spec = pl.BlockSpec((nb, tl, D_MODEL), lambda b, t: (b, t, 0))
    in_specs = [row_spec,
                pl.BlockSpec((None, nb, tl, PLE_DIM), lambda b, t: (layer, b, t, 0)),
                pl.BlockSpec((None, nb, CONV_HALO, D_FF), lambda b, t: (layer, b, 0, 0))]
    in_specs += [_weight_spec(w) for w in lw]
    plans = [_cast_plan(w, layer + 1, (bsz // nb) * nt, nt) for w in cast]
    in_specs += [plan[0] for plan in plans]
    kern = functools.partial(_ffn_kernel, nb=nb, tl=tl, n_cast=len(cast), final_norm=final_norm)
    return pl.pallas_call(
        kern,
        grid=(bsz // nb, nt),
        in_specs=in_specs,
        out_specs=([row_spec, pl.BlockSpec((nb, CONV_HALO, D_FF), lambda b, t: (b, 0, 0))]
                   + [plan[1] for plan in plans]),
        out_shape=([jax.ShapeDtypeStruct((bsz, seq, D_MODEL), F32),
                    jax.ShapeDtypeStruct((bsz, CONV_HALO, D_FF), F32)] + [plan[2] for plan in plans]),
        scratch_shapes=[pltpu.VMEM((nb, CONV_HALO, D_FF), F32),
                        pltpu.VMEM((m, D_MODEL), BF16),
                        pltpu.VMEM((m, D_FF), BF16)],
        compiler_params=pltpu.CompilerParams(
            dimension_semantics=("arbitrary", "arbitrary"), vmem_limit_bytes=VMEM_LIMIT_BYTES),
        name="ffn",
    )(x, p, conv0, *[w[0] for w in lw], *cast)


PROMPT_TILE = 512
PROMPT_FFN_TILE = 1024
PROMPT_GROUP = 256


def kernel(x_prompt, x_sample, cache_pool, cache_k, cache_v, cache_ffn_conv, p_prompt, p_sample,
           g_mix, w_in, b_gate, w_pool_grp, pool_scale, rel_bias, w_pool_proj, w_attn_proj, w_out,
           g_ffn, w_up, w_dw, b_dw, w_down, g_ple, w_ple, w_ple_gate, g_final):
    depth = w_in.shape[0]
    bp = x_prompt.shape[0]
    bs, ls, _ = x_sample.shape

    bf = lambda a: a.astype(BF16)
    row2 = lambda a: a[:, None, :]
    g_mix2, b_gate2, pscale2, g_ffn2, b_dw2, g_ple2 = map(row2, (g_mix, b_gate, pool_scale, g_ffn, b_dw, g_ple))
    g_fin2 = jnp.broadcast_to(g_final[None, None, :], (depth, 1, D_MODEL))
    mix_f32 = [w_in, w_pool_grp.reshape(depth, D_POOL, POOL_GRP), w_pool_proj, w_attn_proj, w_out]
    ffn_f32 = [w_up, w_down, w_ple, w_ple_gate]
    mix_b = [bf(w[0]) for w in mix_f32]

    def mixer_weights(i, mb, diag):
        win_b, wgrp_b, wpp_b, wap_b, wout_b = mb
        wgrp_b = wgrp_b.reshape(len(POOL_WINDOWS), POOL_GRP, POOL_GRP)
        return [(g_mix2, i), (win_b, None), (b_gate2, i), (wgrp_b, None), (pscale2, i), (wpp_b, None),
                (wap_b, None), (wout_b, None), (diag, i)]

    def ffn_weights(i, fb):
        wup_b, wdown_b, wple_b, wpg_b = fb
        return [(g_ffn2, i), (wup_b, None), (w_dw, i), (b_dw2, i), (wdown_b, None), (g_ple2, i), (wple_b, None),
                (wpg_b, None), (g_fin2, i)]

    diag_p = _band_diag(rel_bias, PROMPT_GROUP)
    diag_s = _band_diag(rel_bias, ls)

    pool0_s = jnp.pad(cache_pool, ((0, 0), (0, 0), (POOL_HALO - POOL_PAD, 0), (0, 0)))
    k0_s = bf(cache_k.reshape(depth, bs, BAND_PAST, D_ATTN))
    v0_s = bf(cache_v.reshape(depth, bs, BAND_PAST, D_ATTN))
    conv0_s = jnp.pad(cache_ffn_conv, ((0, 0), (0, 0), (CONV_HALO - cache_ffn_conv.shape[2], 0), (0, 0)))
    conv0_p = jnp.zeros((depth, bp, CONV_HALO, D_FF), F32)

    xp, xs = x_prompt, x_sample
    outs_p, outs_s = [], []
    for i in range(depth):
        fin = i == depth - 1
        xp, npool, nk, nv, *ffn_b = _mixer(xp, None, mixer_weights(i, mix_b, diag_p), i, ffn_f32,
                                           tl=PROMPT_TILE, grp=PROMPT_GROUP, pos0=0)
        xp, nconv, *next_mix_b = _ffn(xp, p_prompt, conv0_p, ffn_weights(i, ffn_b), i, [] if fin else mix_f32,
                                      tl=PROMPT_FFN_TILE, nb=1, final_norm=fin)
        outs_p.append((npool, nk, nv, nconv))
        xs, npool, nk, nv = _mixer(xs, (pool0_s, k0_s, v0_s), mixer_weights(i, mix_b, diag_s), i, [],
                                   tl=ls, grp=ls, pos0=PAST_LEN)
        xs, nconv = _ffn(xs, p_sample, conv0_s, ffn_weights(i, ffn_b), i, [], tl=ls, nb=bs, final_norm=fin)
        outs_s.append((npool, nk, nv, nconv))
        mix_b = next_mix_b

    def assemble(outs, bsz):
        npool = jnp.stack([o[0][:, POOL_HALO - POOL_PAD:, :] for o in outs])
        nk = jnp.stack([o[1].reshape(bsz, -1, N_HEADS, HEAD_DIM) for o in outs])
        nv = jnp.stack([o[2].reshape(bsz, -1, N_HEADS, HEAD_DIM) for o in outs])
        nconv = jnp.stack([o[3][:, CONV_HALO - 2:, :] for o in outs])
        return npool, nk, nv, nconv

    pp, kp, vp, cp = assemble(outs_p, bp)
    ps, ks, vs, cs = assemble(outs_s, bs)
    return (xp, xs, pp, kp, vp, cp, ps, ks, vs, cs)
```

```python
import functools
import math

import jax
import jax.numpy as jnp
import numpy as np
from jax import lax
from jax.experimental import pallas as pl
from jax.experimental.pallas import tpu as pltpu

D_MODEL = 1024
CHUNK = 64
BAND_PAST = 512
D_ATTN = 512
PAST_LEN = 2048
HEAD_DIM = 64
N_HEADS = 8
REL_CLIP = 128
D_POOL = 512
POOL_WINDOWS = (2, 4, 8, 16)
POOL_GRP = 128
POOL_PAD = 15
D_FF = 2816
PLE_DIM = 256
EPS = 1e-6
NEG_INF = -1e30
LOG2E = math.log2(math.e)

LANES = 128
SUBLANES = 8
MXU_DIM = 256
VMEM_LIMIT_BYTES = 56 * 1024 * 1024

POOL_HALO = 16
CONV_HALO = SUBLANES
FF_CHUNK = MXU_DIM
N_FF_CHUNKS = D_FF // FF_CHUNK
HEADS_PER_VREG = LANES // HEAD_DIM

BF16 = jnp.bfloat16
F32 = jnp.float32


def _rms(x, g):
    return x * lax.rsqrt(jnp.mean(x * x, axis=-1, keepdims=True) + EPS) * g


def _dot(a, b):
    return jnp.dot(a, b, preferred_element_type=F32)


def _layer_spec(arr, layer):
    shape = arr.shape[1:]
    nd = len(shape)
    return pl.BlockSpec((None,) + shape, lambda b, t: (layer,) + (0,) * nd, pipeline_mode=pl.Buffered(1))


def _mixer_kernel(*refs, nb, tl, nt, grp, kband, pos0, has_cache, kv_slab, n_prev):
    it = iter(refs)
    x_ref = next(it)
    if has_cache:
        pool0_ref, k0_ref, v0_ref = next(it), next(it), next(it)
    gmix_ref, win_ref, bgate_ref, wgrp_ref, pscale_ref, wpp_ref, wap_ref, wout_ref, diag_ref = (
        next(it) for _ in range(9))
    for _ in range(n_prev):
        next(it)
    (xo_ref, npool_ref, nk_ref, nv_ref,
     h_sc, est, kst, vst, qa_sc, qb_sc, d_sc, att_sc, tab_ref, *kv_sc) = it

    m = nb * tl
    t = pl.program_id(1)
    srows = kst.shape[1]

    @pl.when((pl.program_id(0) == 0) & (t == 0))
    def _build_band_table():
        width = diag_ref.shape[1]
        i = lax.broadcasted_iota(jnp.int32, (grp, kband), 0)
        j = lax.broadcasted_iota(jnp.int32, (grp, kband), 1)
        lo = jnp.bitwise_and(i, -CHUNK)
        valid = (j >= lo) & (j < lo + BAND_PAST + CHUNK) & (j < BAND_PAST + grp)
        for hd in range(N_HEADS):
            d = jnp.broadcast_to(diag_ref[hd:hd + 1, :], (grp, width))
            shifted = pltpu.roll(d, width - (grp - 1), 1, stride=1, stride_axis=0)
            tab_ref[hd] = jnp.where(valid, shifted[:, 0:kband], NEG_INF)

    @pl.when(t == 0)
    def _init_state():
        if has_cache:
            est[:, 0:POOL_HALO, :] = pool0_ref[...]
            kst[:, 0:BAND_PAST, :] = k0_ref[...]
            vst[:, 0:BAND_PAST, :] = v0_ref[...]
        else:
            est[:, 0:POOL_HALO, :] = jnp.zeros((nb, POOL_HALO, D_POOL), F32)
            kst[:, 0:BAND_PAST, :] = jnp.zeros((nb, BAND_PAST, D_ATTN), BF16)
            vst[:, 0:BAND_PAST, :] = jnp.zeros((nb, BAND_PAST, D_ATTN), BF16)
        if srows > BAND_PAST + tl:
            pad = srows - BAND_PAST - tl
            kst[:, BAND_PAST + tl:, :] = jnp.zeros((nb, pad, D_ATTN), BF16)
            vst[:, BAND_PAST + tl:, :] = jnp.zeros((nb, pad, D_ATTN), BF16)

    x = x_ref[...].reshape(m, D_MODEL)
    h = _rms(x, gmix_ref[...]).astype(BF16)
    h_sc[...] = h
    u = _dot(h, win_ref[:, 0:D_POOL])
    q = _dot(h, win_ref[:, D_POOL:D_POOL + D_ATTN]) * (HEAD_DIM ** -0.5 * LOG2E)
    k = _dot(h, win_ref[:, D_POOL + D_ATTN:D_POOL + 2 * D_ATTN])
    v = _dot(h, win_ref[:, D_POOL + 2 * D_ATTN:D_POOL + 3 * D_ATTN])

    lane = lax.broadcasted_iota(jnp.int32, (m, D_ATTN), 1)
    first_head = (lane % LANES) < HEAD_DIM
    qa_sc[...] = jnp.where(first_head, q, 0.0).astype(BF16)
    qb_sc[...] = jnp.where(first_head, 0.0, q).astype(BF16)

    kst[:, BAND_PAST:BAND_PAST + tl, :] = k.astype(BF16).reshape(nb, tl, D_ATTN)
    vst[:, BAND_PAST:BAND_PAST + tl, :] = v.astype(BF16).reshape(nb, tl, D_ATTN)

    if kv_slab is None:
        nk_ref[...] = k.reshape(nb, tl, D_ATTN)
        nv_ref[...] = v.reshape(nb, tl, D_ATTN)
    else:
        kv_sc[0][...] = k
        kv_sc[1][...] = v

    row = lax.broadcasted_iota(jnp.int32, (tl, POOL_GRP), 0)
    for b in range(nb):
        est[b, POOL_HALO:POOL_HALO + tl, :] = u[b * tl:(b + 1) * tl, :]
        e = est[b]
        for g, w in enumerate(POOL_WINDOWS):
            eg = e[:, g * POOL_GRP:(g + 1) * POOL_GRP]
            s = eg
            sh = 1
            while sh < w:
                s = s + pltpu.roll(s, sh, 0)
                sh *= 2
            if pos0 >= POOL_PAD:
                mean = s[POOL_HALO:, :] * (1.0 / w)
            else:
                cnt = jnp.minimum(pos0 + t * tl + row + 1, w).astype(F32)
                mean = s[POOL_HALO:, :] / cnt
            dg = mean - eg[POOL_HALO:, :]
            d_sc[b * tl:(b + 1) * tl, g * POOL_GRP:(g + 1) * POOL_GRP] = dg.astype(BF16)
        tail = e[tl:tl + POOL_HALO, :]
        est[b, 0:POOL_HALO, :] = tail
        npool_ref[b] = tail

    n_grp = tl // grp
    olane = lax.broadcasted_iota(jnp.int32, (grp, LANES), 1)
    for g in range(n_grp):
        n_early = min(kband, -(-max(0, BAND_PAST - pos0 - g * grp) // LANES) * LANES)
        if n_early:
            kcol = lax.broadcasted_iota(jnp.int32, (1, n_early), 1)
            first_pos = pos0 + t * tl + g * grp - BAND_PAST
            kmask = jnp.where(kcol + first_pos >= 0, 0.0, NEG_INF).astype(F32)
        for hp in range(N_HEADS // HEADS_PER_VREG):
            cols = slice(hp * LANES, (hp + 1) * LANES)
            ss = []
            for b in range(nb):
                r0 = b * tl + g * grp
                qs = jnp.concatenate([qa_sc[r0:r0 + grp, cols], qb_sc[r0:r0 + grp, cols]], axis=0)
                k2 = kst[b, g * grp:g * grp + kband, cols]
                ss.append(lax.dot_general(qs, k2, (((1,), (1,)), ((), ())), preferred_element_type=F32))
            tab2 = [tab_ref[hp * HEADS_PER_VREG + j] for j in range(HEADS_PER_VREG)]
            s = jnp.concatenate(ss, axis=0) + jnp.concatenate(tab2 * nb, axis=0)
            if n_early == kband:
                s = s + kmask
            elif n_early:
                s = jnp.concatenate([s[:, :n_early] + kmask, s[:, n_early:]], axis=1)
            p = jnp.exp2(s - jnp.max(s, axis=-1, keepdims=True))
            inv = 1.0 / jnp.sum(p, axis=-1, keepdims=True)
            p = p.astype(BF16)
            for b in range(nb):
                r0 = b * tl + g * grp
                rows = slice(b * 2 * grp, (b + 1) * 2 * grp)
                v2 = vst[b, g * grp:g * grp + kband, cols]
                o = _dot(p[rows, :], v2) * inv[rows, :]
                att_sc[r0:r0 + grp, cols] = jnp.where(olane < HEAD_DIM, o[0:grp, :], o[grp:, :]).astype(BF16)

    ys = [_dot(d_sc[:, g * POOL_GRP:(g + 1) * POOL_GRP], wgrp_ref[g]) for g in range(len(POOL_WINDOWS))]
    y = (jnp.concatenate(ys, axis=1) * pscale_ref[...]).astype(BF16)
    g_off = D_POOL + 3 * D_ATTN
    gate_p = jax.nn.sigmoid(_dot(h_sc[...], win_ref[:, g_off:g_off + D_MODEL]) + bgate_ref[:, 0:D_MODEL])
    merged = gate_p * _dot(y, wpp_ref[...])
    gate_a = jax.nn.sigmoid(_dot(h_sc[...], win_ref[:, g_off + D_MODEL:g_off + 2 * D_MODEL])
                            + bgate_ref[:, D_MODEL:2 * D_MODEL])
    merged = (merged + gate_a * _dot(att_sc[...], wap_ref[...])).astype(BF16)
    xo = x_ref[...].reshape(m, D_MODEL) + _dot(merged, wout_ref[...])
    xo_ref[...] = xo.reshape(nb, tl, D_MODEL)

    if nt > 1:
        kst[:, 0:BAND_PAST, :] = kst[:, tl:tl + BAND_PAST, :]
        vst[:, 0:BAND_PAST, :] = vst[:, tl:tl + BAND_PAST, :]

    if kv_slab is not None:
        @pl.when(t == nt - 1)
        def _emit_kv():
            for ref, sc in ((nk_ref, kv_sc[0]), (nv_ref, kv_sc[1])):
                kt = sc[...].T.reshape(nb, D_ATTN, tl)
                if n_prev:
                    ref[...] = kt
                else:
                    layer, depth = kv_slab
                    for l in range(depth):
                        ref[l] = kt if l == layer else jnp.zeros_like(kt)


def _key_band(grp):
    return -(-(BAND_PAST + grp) // LANES) * LANES


def _band_diag(rel_bias, grp):
    span = grp + _key_band(grp) - 1
    width = -(-span // LANES) * LANES
    rel = np.clip((grp - 1) - np.arange(span) + BAND_PAST, -REL_CLIP, REL_CLIP) + REL_CLIP
    diag = rel_bias[:, :, rel].astype(F32) * LOG2E
    return jnp.pad(diag, ((0, 0), (0, 0), (0, width - span)))


def _mixer(x, cache, lw, layer, prev_kv=None, stack_kv=False, *, tl, grp, pos0):
    bsz, seq, _ = x.shape
    depth = lw[0].shape[0]
    has_cache = cache is not None
    nb = bsz if has_cache else 1
    nt = seq // tl
    assert seq % tl == 0 and tl % grp == 0 and tl % (2 * SUBLANES) == 0
    assert not (has_cache and nt != 1)
    assert not (stack_kv and nb != 1)
    kband = _key_band(grp)
    srows = max(BAND_PAST + tl, (tl // grp - 1) * grp + kband)
    m = nb * tl

    row_spec = pl.BlockSpec((nb, tl, D_MODEL), lambda b, t: (b, t, 0))
    in_specs = [row_spec]
    args = [x]
    if has_cache:
        in_specs += [pl.BlockSpec((None, nb, POOL_HALO, D_POOL), lambda b, t: (layer, b, 0, 0)),
                     pl.BlockSpec((None, nb, BAND_PAST, D_ATTN), lambda b, t: (layer, b, 0, 0)),
                     pl.BlockSpec((None, nb, BAND_PAST, D_ATTN), lambda b, t: (layer, b, 0, 0))]
        args += list(cache)
    in_specs += [_layer_spec(w, layer) for w in lw]
    args += list(lw)
    aliases = {}
    if prev_kv is not None:
        for n, arr in enumerate(prev_kv):
            aliases[len(args)] = 2 + n
            in_specs.append(pl.BlockSpec(memory_space=pl.ANY))
            args.append(arr)

    if not stack_kv:
        kv_shape = jax.ShapeDtypeStruct((bsz, tl, D_ATTN), F32)
        kv_spec = pl.BlockSpec((nb, tl, D_ATTN), lambda b, t: (b, 0, 0))
    else:
        kv_shape = jax.ShapeDtypeStruct((depth, bsz, D_ATTN, tl), F32)
        if prev_kv is None:
            kv_spec = pl.BlockSpec((depth, nb, D_ATTN, tl), lambda b, t: (0, b, 0, 0))
        else:
            kv_spec = pl.BlockSpec((None, nb, D_ATTN, tl), lambda b, t: (layer, b, 0, 0))
    out_shape = [jax.ShapeDtypeStruct((bsz, seq, D_MODEL), F32),
                 jax.ShapeDtypeStruct((bsz, POOL_HALO, D_POOL), F32), kv_shape, kv_shape]
    out_specs = [row_spec, pl.BlockSpec((nb, POOL_HALO, D_POOL), lambda b, t: (b, 0, 0)), kv_spec, kv_spec]
    scratch = [pltpu.VMEM((m, D_MODEL), BF16),
               pltpu.VMEM((nb, POOL_HALO + tl, D_POOL), F32),
               pltpu.VMEM((nb, srows, D_ATTN), BF16),
               pltpu.VMEM((nb, srows, D_ATTN), BF16),
               pltpu.VMEM((m, D_ATTN), BF16),
               pltpu.VMEM((m, D_ATTN), BF16),
               pltpu.VMEM((m, D_POOL), BF16),
               pltpu.VMEM((m, D_ATTN), BF16),
               pltpu.VMEM((N_HEADS, grp, kband), F32)]
    if stack_kv:
        scratch += [pltpu.VMEM((m, D_ATTN), F32)] * 2
    kern = functools.partial(_mixer_kernel, nb=nb, tl=tl, nt=nt, grp=grp, kband=kband,
                             pos0=pos0, has_cache=has_cache, kv_slab=(layer, depth) if stack_kv else None,
                             n_prev=len(aliases))
    return pl.pallas_call(
        kern,
        grid=(bsz // nb, nt),
        in_specs=in_specs,
        out_specs=out_specs,
        out_shape=out_shape,
        scratch_shapes=scratch,
        input_output_aliases=aliases,
        compiler_params=pltpu.CompilerParams(
            dimension_semantics=("arbitrary", "arbitrary"), vmem_limit_bytes=VMEM_LIMIT_BYTES),
        name="mixer_cached" if has_cache else "mixer",
    )(*args)


def _ffn_kernel(x_ref, p_ref, conv0_ref, gffn_ref, wup_ref, wdw_ref, bdw_ref, wdown_ref,
                gple_ref, wple_ref, wpg_ref, gfin_ref,
                xo_ref, nconv_ref, cst, h_sc, y_sc, *, nb, tl, final_norm):
    m = nb * tl
    t = pl.program_id(1)

    @pl.when(t == 0)
    def _init_state():
        cst[...] = conv0_ref[...]

    x = x_ref[...].reshape(m, D_MODEL)
    h_sc[...] = _rms(x, gffn_ref[...]).astype(BF16)
    row = lax.broadcasted_iota(jnp.int32, (tl, FF_CHUNK), 0)
    for j in range(N_FF_CHUNKS):
        cols = slice(j * FF_CHUNK, (j + 1) * FF_CHUNK)
        a_all = _dot(h_sc[...], wup_ref[:, cols])
        g_all = _dot(h_sc[...], wup_ref[:, D_FF + j * FF_CHUNK:D_FF + (j + 1) * FF_CHUNK])
        w0, w1, w2 = wdw_ref[0:1, cols], wdw_ref[1:2, cols], wdw_ref[2:3, cols]
        bias = bdw_ref[:, cols]
        for b in range(nb):
            a = a_all[b * tl:(b + 1) * tl, :]
            gate = g_all[b * tl:(b + 1) * tl, :]
            c6 = cst[b, CONV_HALO - 2:CONV_HALO - 1, cols]
            c7 = cst[b, CONV_HALO - 1:CONV_HALO, cols]
            a1 = jnp.where(row == 0, c7, pltpu.roll(a, 1, 0))
            a2 = jnp.where(row == 0, c6, jnp.where(row == 1, c7, pltpu.roll(a, 2, 0)))
            conv = bias + a2 * w0 + a1 * w1 + a * w2
            y_sc[b * tl:(b + 1) * tl, cols] = (jax.nn.gelu(conv) * gate).astype(BF16)
            tail = a[tl - CONV_HALO:tl, :]
            cst[b, :, cols] = tail
            nconv_ref[b, :, cols] = tail

    x2 = x_ref[...].reshape(m, D_MODEL) + _dot(y_sc[...], wdown_ref[...])
    hp = _rms(x2, gple_ref[...]).astype(BF16)
    gate = jax.nn.sigmoid(_dot(hp, wpg_ref[...]))
    pe = _dot(p_ref[...].reshape(m, PLE_DIM).astype(BF16), wple_ref[...])
    x3 = x2 + pe * gate
    if final_norm:
        x3 = _rms(x3, gfin_ref[...])
    xo_ref[...] = x3.reshape(nb, tl, D_MODEL)


def _ffn(x, p, conv0, lw, layer, *, tl, nb, final_norm):
    bsz, seq, _ = x.shape
    nt = seq // tl
    assert seq % tl == 0 and bsz % nb == 0 and tl % CONV_HALO == 0
    m = nb * tl
    weights = list(lw)
    row_spec = pl.BlockSpec((nb, tl, D_MODEL), lambda b, t: (b, t, 0))
    in_specs = [row_spec,
                pl.BlockSpec((None, nb, tl, PLE_DIM), lambda b, t: (layer, b, t, 0)),
                pl.BlockSpec((None, nb, CONV_HALO, D_FF), lambda b, t: (layer, b, 0, 0))]
    in_specs += [_layer_spec(w, layer) for w in weights]
    kern = functools.partial(_ffn_kernel, nb=nb, tl=tl, final_norm=final_norm)
    return pl.pallas_call(
        kern,
        grid=(bsz // nb, nt),
        in_specs=in_specs,
        out_specs=[row_spec, pl.BlockSpec((nb, CONV_HALO, D_FF), lambda b, t: (b, 0, 0))],
        out_shape=[jax.ShapeDtypeStruct((bsz, seq, D_MODEL), F32),
                   jax.ShapeDtypeStruct((bsz, CONV_HALO, D_FF), F32)],
        scratch_shapes=[pltpu.VMEM((nb, CONV_HALO, D_FF), F32),
                        pltpu.VMEM((m, D_MODEL), BF16),
                        pltpu.VMEM((m, D_FF), BF16)],
        compiler_params=pltpu.CompilerParams(
            dimension_semantics=("arbitrary", "arbitrary"), vmem_limit_bytes=VMEM_LIMIT_BYTES),
        name="ffn",
    )(x, p, conv0, *weights)


PROMPT_TILE = 512
PROMPT_FFN_TILE = 1024
PROMPT_GROUP = 256


def kernel(x_prompt, x_sample, cache_pool, cache_k, cache_v, cache_ffn_conv, p_prompt, p_sample,
           g_mix, w_in, b_gate, w_pool_grp, pool_scale, rel_bias, w_pool_proj, w_attn_proj, w_out,
           g_ffn, w_up, w_dw, b_dw, w_down, g_ple, w_ple, w_ple_gate, g_final):
    depth = w_in.shape[0]
    bp = x_prompt.shape[0]
    bs, ls, _ = x_sample.shape

    bf = lambda a: a.astype(BF16)
    row2 = lambda a: a[:, None, :]
    mix_w = [row2(g_mix), bf(w_in), row2(b_gate), bf(w_pool_grp), row2(pool_scale), bf(w_pool_proj),
             bf(w_attn_proj), bf(w_out)]
    ffn_w = [row2(g_ffn), bf(w_up), w_dw, row2(b_dw), bf(w_down), row2(g_ple), bf(w_ple), bf(w_ple_gate),
             jnp.broadcast_to(g_final[None, None, :], (depth, 1, D_MODEL))]
    diag_p = _band_diag(rel_bias, PROMPT_GROUP)
    diag_s = _band_diag(rel_bias, ls)

    pool0_s = jnp.pad(cache_pool, ((0, 0), (0, 0), (POOL_HALO - POOL_PAD, 0), (0, 0)))
    k0_s = bf(cache_k.reshape(depth, bs, BAND_PAST, D_ATTN))
    v0_s = bf(cache_v.reshape(depth, bs, BAND_PAST, D_ATTN))
    conv0_s = jnp.pad(cache_ffn_conv, ((0, 0), (0, 0), (CONV_HALO - cache_ffn_conv.shape[2], 0), (0, 0)))
    conv0_p = jnp.zeros((depth, bp, CONV_HALO, D_FF), F32)

    xp, xs = x_prompt, x_sample
    outs_p, outs_s = [], []
    kv_p = None
    for i in range(depth):
        fin = i == depth - 1
        xp, npool, *kv_p = _mixer(xp, None, mix_w + [diag_p], i, kv_p, True,
                                  tl=PROMPT_TILE, grp=PROMPT_GROUP, pos0=0)
        xp, nconv = _ffn(xp, p_prompt, conv0_p, ffn_w, i, tl=PROMPT_FFN_TILE, nb=1, final_norm=fin)
        outs_p.append((npool, nconv))
        xs, npool, nk, nv = _mixer(xs, (pool0_s, k0_s, v0_s), mix_w + [diag_s], i, tl=ls, grp=ls, pos0=PAST_LEN)
        xs, nconv = _ffn(xs, p_sample, conv0_s, ffn_w, i, tl=ls, nb=bs, final_norm=fin)
        outs_s.append((npool, nconv, nk, nv))

    def pool_conv(outs):
        return (jnp.stack([o[0][:, POOL_HALO - POOL_PAD:, :] for o in outs]),
                jnp.stack([o[1][:, CONV_HALO - 2:, :] for o in outs]))

    def by_head(kv_t):
        return kv_t.reshape(depth, bp, N_HEADS, HEAD_DIM, -1).transpose(0, 1, 4, 2, 3)

    pp, cp = pool_conv(outs_p)
    ps, cs = pool_conv(outs_s)
    ks = jnp.stack([o[2].reshape(bs, -1, N_HEADS, HEAD_DIM) for o in outs_s])
    vs = jnp.stack([o[3].reshape(bs, -1, N_HEADS, HEAD_DIM) for o in outs_s])
    return (xp, xs, pp, by_head(kv_p[0]), by_head(kv_p[1]), cp, ps, ks, vs, cs)
```

```python
import functools
import math

import jax
import jax.numpy as jnp
import numpy as np
from jax import lax
from jax.experimental import pallas as pl
from jax.experimental.pallas import tpu as pltpu

D_MODEL = 1024
CHUNK = 64
BAND_PAST = 512
D_ATTN = 512
PAST_LEN = 2048
HEAD_DIM = 64
N_HEADS = 8
REL_CLIP = 128
D_POOL = 512
POOL_WINDOWS = (2, 4, 8, 16)
POOL_GRP = 128
POOL_PAD = 15
D_FF = 2816
PLE_DIM = 256
EPS = 1e-6
NEG_INF = -1e30
LOG2E = math.log2(math.e)

LANES = 128
SUBLANES = 8
MXU_DIM = 256
VMEM_LIMIT_BYTES = 56 * 1024 * 1024

POOL_HALO = 16
CONV_HALO = SUBLANES
FF_CHUNK = MXU_DIM
N_FF_CHUNKS = D_FF // FF_CHUNK
HEADS_PER_VREG = LANES // HEAD_DIM

BF16 = jnp.bfloat16
F32 = jnp.float32


def _rms(x, g):
    return x * lax.rsqrt(jnp.mean(x * x, axis=-1, keepdims=True) + EPS) * g


def _dot(a, b):
    return jnp.dot(a, b, preferred_element_type=F32)


def _layer_spec(arr, layer):
    shape = arr.shape[1:]
    nd = len(shape)
    return pl.BlockSpec((None,) + shape, lambda b, t: (layer,) + (0,) * nd, pipeline_mode=pl.Buffered(1))


def _mixer_kernel(*refs, nb, tl, nt, grp, kband, pos0, has_cache, kv_slab, n_prev):
    it = iter(refs)
    x_ref = next(it)
    if has_cache:
        pool0_ref, k0_ref, v0_ref = next(it), next(it), next(it)
    gmix_ref, win_ref, bgate_ref, wgrp_ref, pscale_ref, wpp_ref, wap_ref, wout_ref, diag_ref = (
        next(it) for _ in range(9))
    for _ in range(n_prev):
        next(it)
    (xo_ref, npool_ref, nk_ref, nv_ref,
     h_sc, est, kst, vst, qa_sc, qb_sc, d_sc, att_sc, tab_ref, *kv_sc) = it

    m = nb * tl
    t = pl.program_id(1)
    srows = kst.shape[1]

    @pl.when((pl.program_id(0) == 0) & (t == 0))
    def _build_band_table():
        width = diag_ref.shape[1]
        i = lax.broadcasted_iota(jnp.int32, (grp, kband), 0)
        j = lax.broadcasted_iota(jnp.int32, (grp, kband), 1)
        lo = jnp.bitwise_and(i, -CHUNK)
        valid = (j >= lo) & (j < lo + BAND_PAST + CHUNK) & (j < BAND_PAST + grp)
        for hd in range(N_HEADS):
            d = jnp.broadcast_to(diag_ref[hd:hd + 1, :], (grp, width))
            shifted = pltpu.roll(d, width - (grp - 1), 1, stride=1, stride_axis=0)
            tab_ref[hd] = jnp.where(valid, shifted[:, 0:kband], NEG_INF)

    @pl.when(t == 0)
    def _init_state():
        if has_cache:
            est[:, 0:POOL_HALO, :] = pool0_ref[...]
            for b in range(nb):
                kst[b, 0:BAND_PAST, :] = k0_ref[b].T.astype(BF16)
                vst[b, 0:BAND_PAST, :] = v0_ref[b].T.astype(BF16)
        else:
            est[:, 0:POOL_HALO, :] = jnp.zeros((nb, POOL_HALO, D_POOL), F32)
            kst[:, 0:BAND_PAST, :] = jnp.zeros((nb, BAND_PAST, D_ATTN), BF16)
            vst[:, 0:BAND_PAST, :] = jnp.zeros((nb, BAND_PAST, D_ATTN), BF16)
        if srows > BAND_PAST + tl:
            pad = srows - BAND_PAST - tl
            kst[:, BAND_PAST + tl:, :] = jnp.zeros((nb, pad, D_ATTN), BF16)
            vst[:, BAND_PAST + tl:, :] = jnp.zeros((nb, pad, D_ATTN), BF16)

    x = x_ref[...].reshape(m, D_MODEL)
    h = _rms(x, gmix_ref[...]).astype(BF16)
    h_sc[...] = h
    u = _dot(h, win_ref[:, 0:D_POOL])
    q = _dot(h, win_ref[:, D_POOL:D_POOL + D_ATTN]) * (HEAD_DIM ** -0.5 * LOG2E)
    k = _dot(h, win_ref[:, D_POOL + D_ATTN:D_POOL + 2 * D_ATTN])
    v = _dot(h, win_ref[:, D_POOL + 2 * D_ATTN:D_POOL + 3 * D_ATTN])

    lane = lax.broadcasted_iota(jnp.int32, (m, D_ATTN), 1)
    first_head = (lane % LANES) < HEAD_DIM
    qa_sc[...] = jnp.where(first_head, q, 0.0).astype(BF16)
    qb_sc[...] = jnp.where(first_head, 0.0, q).astype(BF16)

    kst[:, BAND_PAST:BAND_PAST + tl, :] = k.astype(BF16).reshape(nb, tl, D_ATTN)
    vst[:, BAND_PAST:BAND_PAST + tl, :] = v.astype(BF16).reshape(nb, tl, D_ATTN)

    if kv_slab is None:
        nk_ref[...] = k.reshape(nb, tl, D_ATTN)
        nv_ref[...] = v.reshape(nb, tl, D_ATTN)
    else:
        kv_sc[0][...] = k
        kv_sc[1][...] = v

    row = lax.broadcasted_iota(jnp.int32, (tl, POOL_GRP), 0)
    for b in range(nb):
        est[b, POOL_HALO:POOL_HALO + tl, :] = u[b * tl:(b + 1) * tl, :]
        e = est[b]
        for g, w in enumerate(POOL_WINDOWS):
            eg = e[:, g * POOL_GRP:(g + 1) * POOL_GRP]
            s = eg
            sh = 1
            while sh < w:
                s = s + pltpu.roll(s, sh, 0)
                sh *= 2
            if pos0 >= POOL_PAD:
                mean = s[POOL_HALO:, :] * (1.0 / w)
            else:
                cnt = jnp.minimum(pos0 + t * tl + row + 1, w).astype(F32)
                mean = s[POOL_HALO:, :] / cnt
            dg = mean - eg[POOL_HALO:, :]
            d_sc[b * tl:(b + 1) * tl, g * POOL_GRP:(g + 1) * POOL_GRP] = dg.astype(BF16)
        tail = e[tl:tl + POOL_HALO, :]
        est[b, 0:POOL_HALO, :] = tail
        npool_ref[b] = tail

    n_grp = tl // grp
    olane = lax.broadcasted_iota(jnp.int32, (grp, LANES), 1)
    for g in range(n_grp):
        n_early = min(kband, -(-max(0, BAND_PAST - pos0 - g * grp) // LANES) * LANES)
        if n_early:
            kcol = lax.broadcasted_iota(jnp.int32, (1, n_early), 1)
            first_pos = pos0 + t * tl + g * grp - BAND_PAST
            kmask = jnp.where(kcol + first_pos >= 0, 0.0, NEG_INF).astype(F32)
        for hp in range(N_HEADS // HEADS_PER_VREG):
            cols = slice(hp * LANES, (hp + 1) * LANES)
            ss = []
            for b in range(nb):
                r0 = b * tl + g * grp
                qs = jnp.concatenate([qa_sc[r0:r0 + grp, cols], qb_sc[r0:r0 + grp, cols]], axis=0)
                k2 = kst[b, g * grp:g * grp + kband, cols]
                ss.append(lax.dot_general(qs, k2, (((1,), (1,)), ((), ())), preferred_element_type=F32))
            tab2 = [tab_ref[hp * HEADS_PER_VREG + j] for j in range(HEADS_PER_VREG)]
            s = jnp.concatenate(ss, axis=0) + jnp.concatenate(tab2 * nb, axis=0)
            if n_early == kband:
                s = s + kmask
            elif n_early:
                s = jnp.concatenate([s[:, :n_early] + kmask, s[:, n_early:]], axis=1)
            p = jnp.exp2(s - jnp.max(s, axis=-1, keepdims=True))
            inv = 1.0 / jnp.sum(p, axis=-1, keepdims=True)
            p = p.astype(BF16)
            for b in range(nb):
                r0 = b * tl + g * grp
                rows = slice(b * 2 * grp, (b + 1) * 2 * grp)
                v2 = vst[b, g * grp:g * grp + kband, cols]
                o = _dot(p[rows, :], v2) * inv[rows, :]
                att_sc[r0:r0 + grp, cols] = jnp.where(olane < HEAD_DIM, o[0:grp, :], o[grp:, :]).astype(BF16)

    ys = [_dot(d_sc[:, g * POOL_GRP:(g + 1) * POOL_GRP], wgrp_ref[g]) for g in range(len(POOL_WINDOWS))]
    y = (jnp.concatenate(ys, axis=1) * pscale_ref[...]).astype(BF16)
    g_off = D_POOL + 3 * D_ATTN
    gate_p = jax.nn.sigmoid(_dot(h_sc[...], win_ref[:, g_off:g_off + D_MODEL]) + bgate_ref[:, 0:D_MODEL])
    merged = gate_p * _dot(y, wpp_ref[...])
    gate_a = jax.nn.sigmoid(_dot(h_sc[...], win_ref[:, g_off + D_MODEL:g_off + 2 * D_MODEL])
                            + bgate_ref[:, D_MODEL:2 * D_MODEL])
    merged = (merged + gate_a * _dot(att_sc[...], wap_ref[...])).astype(BF16)
    xo = x_ref[...].reshape(m, D_MODEL) + _dot(merged, wout_ref[...])
    xo_ref[...] = xo.reshape(nb, tl, D_MODEL)

    if nt > 1:
        kst[:, 0:BAND_PAST, :] = kst[:, tl:tl + BAND_PAST, :]
        vst[:, 0:BAND_PAST, :] = vst[:, tl:tl + BAND_PAST, :]

    if kv_slab is not None:
        @pl.when(t == nt - 1)
        def _emit_kv():
            for ref, sc in ((nk_ref, kv_sc[0]), (nv_ref, kv_sc[1])):
                kt = sc[...].T.reshape(nb, D_ATTN, tl)
                if n_prev:
                    ref[...] = kt
                else:
                    layer, depth = kv_slab
                    for l in range(depth):
                        ref[l] = kt if l == layer else jnp.zeros_like(kt)


def _key_band(grp):
    return -(-(BAND_PAST + grp) // LANES) * LANES


def _band_diag(rel_bias, grp):
    span = grp + _key_band(grp) - 1
    width = -(-span // LANES) * LANES
    rel = np.clip((grp - 1) - np.arange(span) + BAND_PAST, -REL_CLIP, REL_CLIP) + REL_CLIP
    diag = rel_bias[:, :, rel].astype(F32) * LOG2E
    return jnp.pad(diag, ((0, 0), (0, 0), (0, width - span)))


def _mixer(x, cache, lw, layer, prev_kv=None, stack_kv=False, *, tl, grp, pos0):
    bsz, seq, _ = x.shape
    depth = lw[0].shape[0]
    has_cache = cache is not None
    nb = bsz if has_cache else 1
    nt = seq // tl
    assert seq % tl == 0 and tl % grp == 0 and tl % (2 * SUBLANES) == 0
    assert not (has_cache and nt != 1)
    assert not (stack_kv and nb != 1)
    kband = _key_band(grp)
    srows = max(BAND_PAST + tl, (tl // grp - 1) * grp + kband)
    m = nb * tl

    row_spec = pl.BlockSpec((nb, tl, D_MODEL), lambda b, t: (b, t, 0))
    in_specs = [row_spec]
    args = [x]
    if has_cache:
        kv0_spec = pl.BlockSpec((None, nb, D_ATTN, BAND_PAST), lambda b, t: (layer, b, 0, 0),
                                pipeline_mode=pl.Buffered(1))
        in_specs += [pl.BlockSpec((None, nb, POOL_HALO, D_POOL), lambda b, t: (layer, b, 0, 0)), kv0_spec, kv0_spec]
        args += list(cache)
    in_specs += [_layer_spec(w, layer) for w in lw]
    args += list(lw)
    aliases = {}
    if prev_kv is not None:
        for n, arr in enumerate(prev_kv):
            aliases[len(args)] = 2 + n
            in_specs.append(pl.BlockSpec(memory_space=pl.ANY))
            args.append(arr)

    if not stack_kv:
        kv_shape = jax.ShapeDtypeStruct((bsz, tl, D_ATTN), F32)
        kv_spec = pl.BlockSpec((nb, tl, D_ATTN), lambda b, t: (b, 0, 0))
    else:
        kv_shape = jax.ShapeDtypeStruct((depth, bsz, D_ATTN, tl), F32)
        if prev_kv is None:
            kv_spec = pl.BlockSpec((depth, nb, D_ATTN, tl), lambda b, t: (0, b, 0, 0))
        else:
            kv_spec = pl.BlockSpec((None, nb, D_ATTN, tl), lambda b, t: (layer, b, 0, 0))
    out_shape = [jax.ShapeDtypeStruct((bsz, seq, D_MODEL), F32),
                 jax.ShapeDtypeStruct((bsz, POOL_HALO, D_POOL), F32), kv_shape, kv_shape]
    out_specs = [row_spec, pl.BlockSpec((nb, POOL_HALO, D_POOL), lambda b, t: (b, 0, 0)), kv_spec, kv_spec]
    scratch = [pltpu.VMEM((m, D_MODEL), BF16),
               pltpu.VMEM((nb, POOL_HALO + tl, D_POOL), F32),
               pltpu.VMEM((nb, srows, D_ATTN), BF16),
               pltpu.VMEM((nb, srows, D_ATTN), BF16),
               pltpu.VMEM((m, D_ATTN), BF16),
               pltpu.VMEM((m, D_ATTN), BF16),
               pltpu.VMEM((m, D_POOL), BF16),
               pltpu.VMEM((m, D_ATTN), BF16),
               pltpu.VMEM((N_HEADS, grp, kband), F32)]
    if stack_kv:
        scratch += [pltpu.VMEM((m, D_ATTN), F32)] * 2
    kern = functools.partial(_mixer_kernel, nb=nb, tl=tl, nt=nt, grp=grp, kband=kband,
                             pos0=pos0, has_cache=has_cache, kv_slab=(layer, depth) if stack_kv else None,
                             n_prev=len(aliases))
    return pl.pallas_call(
        kern,
        grid=(bsz // nb, nt),
        in_specs=in_specs,
        out_specs=out_specs,
        out_shape=out_shape,
        scratch_shapes=scratch,
        input_output_aliases=aliases,
        compiler_params=pltpu.CompilerParams(
            dimension_semantics=("arbitrary", "arbitrary"), vmem_limit_bytes=VMEM_LIMIT_BYTES),
        name="mixer_cached" if has_cache else "mixer",
    )(*args)


def _ffn_kernel(x_ref, p_ref, conv0_ref, gffn_ref, wup_ref, wdw_ref, bdw_ref, wdown_ref,
                gple_ref, wple_ref, wpg_ref, gfin_ref,
                xo_ref, nconv_ref, cst, h_sc, y_sc, *, nb, tl, final_norm):
    m = nb * tl
    t = pl.program_id(1)

    @pl.when(t == 0)
    def _init_state():
        cst[...] = conv0_ref[...]

    x = x_ref[...].reshape(m, D_MODEL)
    h_sc[...] = _rms(x, gffn_ref[...]).astype(BF16)
    row = lax.broadcasted_iota(jnp.int32, (tl, FF_CHUNK), 0)
    for j in range(N_FF_CHUNKS):
        cols = slice(j * FF_CHUNK, (j + 1) * FF_CHUNK)
        a_all = _dot(h_sc[...], wup_ref[:, cols])
        g_all = _dot(h_sc[...], wup_ref[:, D_FF + j * FF_CHUNK:D_FF + (j + 1) * FF_CHUNK])
        w0, w1, w2 = wdw_ref[0:1, cols], wdw_ref[1:2, cols], wdw_ref[2:3, cols]
        bias = bdw_ref[:, cols]
        for b in range(nb):
            a = a_all[b * tl:(b + 1) * tl, :]
            gate = g_all[b * tl:(b + 1) * tl, :]
            c6 = cst[b, CONV_HALO - 2:CONV_HALO - 1, cols]
            c7 = cst[b, CONV_HALO - 1:CONV_HALO, cols]
            a1 = jnp.where(row == 0, c7, pltpu.roll(a, 1, 0))
            a2 = jnp.where(row == 0, c6, jnp.where(row == 1, c7, pltpu.roll(a, 2, 0)))
            conv = bias + a2 * w0 + a1 * w1 + a * w2
            y_sc[b * tl:(b + 1) * tl, cols] = (jax.nn.gelu(conv) * gate).astype(BF16)
            tail = a[tl - CONV_HALO:tl, :]
            cst[b, :, cols] = tail
            nconv_ref[b, :, cols] = tail

    x2 = x_ref[...].reshape(m, D_MODEL) + _dot(y_sc[...], wdown_ref[...])
    hp = _rms(x2, gple_ref[...]).astype(BF16)
    gate = jax.nn.sigmoid(_dot(hp, wpg_ref[...]))
    pe = _dot(p_ref[...].reshape(m, PLE_DIM).astype(BF16), wple_ref[...])
    x3 = x2 + pe * gate
    if final_norm:
        x3 = _rms(x3, gfin_ref[...])
    xo_ref[...] = x3.reshape(nb, tl, D_MODEL)


def _ffn(x, p, conv0, lw, layer, *, tl, nb, final_norm):
    bsz, seq, _ = x.shape
    nt = seq // tl
    assert seq % tl == 0 and bsz % nb == 0 and tl % CONV_HALO == 0
    m = nb * tl
    weights = list(lw)
    row_spec = pl.BlockSpec((nb, tl, D_MODEL), lambda b, t: (b, t, 0))
    in_specs = [row_spec,
                pl.BlockSpec((None, nb, tl, PLE_DIM), lambda b, t: (layer, b, t, 0)),
                pl.BlockSpec((None, nb, CONV_HALO, D_FF), lambda b, t: (layer, b, 0, 0))]
    in_specs += [_layer_spec(w, layer) for w in weights]
    kern = functools.partial(_ffn_kernel, nb=nb, tl=tl, final_norm=final_norm)
    return pl.pallas_call(
        kern,
        grid=(bsz // nb, nt),
        in_specs=in_specs,
        out_specs=[row_spec, pl.BlockSpec((nb, CONV_HALO, D_FF), lambda b, t: (b, 0, 0))],
        out_shape=[jax.ShapeDtypeStruct((bsz, seq, D_MODEL), F32),
                   jax.ShapeDtypeStruct((bsz, CONV_HALO, D_FF), F32)],
        scratch_shapes=[pltpu.VMEM((nb, CONV_HALO, D_FF), F32),
                        pltpu.VMEM((m, D_MODEL), BF16),
                        pltpu.VMEM((m, D_FF), BF16)],
        compiler_params=pltpu.CompilerParams(
            dimension_semantics=("arbitrary", "arbitrary"), vmem_limit_bytes=VMEM_LIMIT_BYTES),
        name="ffn",
    )(x, p, conv0, *weights)


PROMPT_TILE = 512
PROMPT_FFN_TILE = 1024
PROMPT_GROUP = 256


def kernel(x_prompt, x_sample, cache_pool, cache_k, cache_v, cache_ffn_conv, p_prompt, p_sample,
           g_mix, w_in, b_gate, w_pool_grp, pool_scale, rel_bias, w_pool_proj, w_attn_proj, w_out,
           g_ffn, w_up, w_dw, b_dw, w_down, g_ple, w_ple, w_ple_gate, g_final):
    depth = w_in.shape[0]
    bp = x_prompt.shape[0]
    bs, ls, _ = x_sample.shape

    bf = lambda a: a.astype(BF16)
    row2 = lambda a: a[:, None, :]
    mix_w = [row2(g_mix), bf(w_in), row2(b_gate), bf(w_pool_grp), row2(pool_scale), bf(w_pool_proj),
             bf(w_attn_proj), bf(w_out)]
    ffn_w = [row2(g_ffn), bf(w_up), w_dw, row2(b_dw), bf(w_down), row2(g_ple), bf(w_ple), bf(w_ple_gate),
             jnp.broadcast_to(g_final[None, None, :], (depth, 1, D_MODEL))]
    diag_p = _band_diag(rel_bias, PROMPT_GROUP)
    diag_s = _band_diag(rel_bias, ls)

    pool0_s = jnp.pad(cache_pool, ((0, 0), (0, 0), (POOL_HALO - POOL_PAD, 0), (0, 0)))
    k0_s = cache_k.transpose(0, 1, 3, 4, 2).reshape(depth, bs, D_ATTN, BAND_PAST)
    v0_s = cache_v.transpose(0, 1, 3, 4, 2).reshape(depth, bs, D_ATTN, BAND_PAST)
    conv0_s = jnp.pad(cache_ffn_conv, ((0, 0), (0, 0), (CONV_HALO - cache_ffn_conv.shape[2], 0), (0, 0)))
    conv0_p = jnp.zeros((depth, bp, CONV_HALO, D_FF), F32)

    xp, xs = x_prompt, x_sample
    outs_p, outs_s = [], []
    kv_p = None
    for i in range(depth):
        fin = i == depth - 1
        xp, npool, *kv_p = _mixer(xp, None, mix_w + [diag_p], i, kv_p, True,
                                  tl=PROMPT_TILE, grp=PROMPT_GROUP, pos0=0)
        xp, nconv = _ffn(xp, p_prompt, conv0_p, ffn_w, i, tl=PROMPT_FFN_TILE, nb=1, final_norm=fin)
        outs_p.append((npool, nconv))
        xs, npool, nk, nv = _mixer(xs, (pool0_s, k0_s, v0_s), mix_w + [diag_s], i, tl=ls, grp=ls, pos0=PAST_LEN)
        xs, nconv = _ffn(xs, p_sample, conv0_s, ffn_w, i, tl=ls, nb=bs, final_norm=fin)
        outs_s.append((npool, nconv, nk, nv))

    def pool_conv(outs):
        return (jnp.stack([o[0][:, POOL_HALO - POOL_PAD:, :] for o in outs]),
                jnp.stack([o[1][:, CONV_HALO - 2:, :] for o in outs]))

    def by_head(kv_t):
        return kv_t.reshape(depth, bp, N_HEADS, HEAD_DIM, -1).transpose(0, 1, 4, 2, 3)

    pp, cp = pool_conv(outs_p)
    ps, cs = pool_conv(outs_s)
    ks = jnp.stack([o[2].reshape(bs, -1, N_HEADS, HEAD_DIM) for o in outs_s])
    vs = jnp.stack([o[3].reshape(bs, -1, N_HEADS, HEAD_DIM) for o in outs_s])
    return (xp, xs, pp, by_head(kv_p[0]), by_head(kv_p[1]), cp, ps, ks, vs, cs)
```

```python
import functools
import math

import jax
import jax.numpy as jnp
import numpy as np
from jax import lax
from jax.experimental import pallas as pl
from jax.experimental.pallas import tpu as pltpu

D_MODEL = 1024
CHUNK = 64
BAND_PAST = 512
D_ATTN = 512
PAST_LEN = 2048
HEAD_DIM = 64
N_HEADS = 8
REL_CLIP = 128
D_POOL = 512
POOL_WINDOWS = (2, 4, 8, 16)
POOL_GRP = 128
POOL_PAD = 15
D_FF = 2816
PLE_DIM = 256
EPS = 1e-6
NEG_INF = -1e30
LOG2E = math.log2(math.e)

LANES = 128
SUBLANES = 8
MXU_DIM = 256
VMEM_LIMIT_BYTES = 56 * 1024 * 1024

POOL_HALO = 16
CONV_HALO = SUBLANES
FF_CHUNK = MXU_DIM
N_FF_CHUNKS = D_FF // FF_CHUNK
HEADS_PER_VREG = LANES // HEAD_DIM

BF16 = jnp.bfloat16
F32 = jnp.float32


def _rms(x, g):
    return x * lax.rsqrt(jnp.mean(x * x, axis=-1, keepdims=True) + EPS) * g


def _dot(a, b):
    return jnp.dot(a, b, preferred_element_type=F32)


def _weight_spec(w):
    arr, layer = w
    if layer is None:
        block, lead = arr.shape, ()
    else:
        block, lead = (None,) + arr.shape[1:], (layer,)
    nd = arr.ndim - len(lead)
    return pl.BlockSpec(block, lambda b, t: lead + (0,) * nd, pipeline_mode=pl.Buffered(1))


def _cast_plan(w, layer, n_steps, nt):
    _, rows, cols = w.shape
    rb = next(r for r in range(2 * SUBLANES, rows + 1, 2 * SUBLANES) if rows % r == 0 and rows // r <= n_steps)
    last = rows // rb - 1
    return (pl.BlockSpec((None, rb, cols), lambda b, t: (layer, jnp.minimum(b * nt + t, last), 0)),
            pl.BlockSpec((rb, cols), lambda b, t: (jnp.minimum(b * nt + t, last), 0)),
            jax.ShapeDtypeStruct((rows, cols), BF16))


def _mixer_kernel(*refs, nb, tl, nt, grp, kband, pos0, has_cache, kv_slab, n_prev, n_cast):
    it = iter(refs)
    x_ref = next(it)
    if has_cache:
        pool0_ref, k0_ref, v0_ref = next(it), next(it), next(it)
    gmix_ref, win_ref, bgate_ref, wgrp_ref, pscale_ref, wpp_ref, wap_ref, wout_ref, diag_ref = (
        next(it) for _ in range(9))
    for _ in range(n_prev):
        next(it)
    cast_in = [next(it) for _ in range(n_cast)]
    xo_ref, npool_ref, nk_ref, nv_ref = (next(it) for _ in range(4))
    cast_out = [next(it) for _ in range(n_cast)]
    h_sc, est, kst, vst, qa_sc, qb_sc, d_sc, att_sc, tab_ref, *kv_sc = it

    m = nb * tl
    t = pl.program_id(1)
    srows = kst.shape[1]

    @pl.when((pl.program_id(0) == 0) & (t == 0))
    def _build_band_table():
        width = diag_ref.shape[1]
        i = lax.broadcasted_iota(jnp.int32, (grp, kband), 0)
        j = lax.broadcasted_iota(jnp.int32, (grp, kband), 1)
        lo = jnp.bitwise_and(i, -CHUNK)
        valid = (j >= lo) & (j < lo + BAND_PAST + CHUNK) & (j < BAND_PAST + grp)
        for hd in range(N_HEADS):
            d = jnp.broadcast_to(diag_ref[hd:hd + 1, :], (grp, width))
            shifted = pltpu.roll(d, width - (grp - 1), 1, stride=1, stride_axis=0)
            tab_ref[hd] = jnp.where(valid, shifted[:, 0:kband], NEG_INF)

    @pl.when(t == 0)
    def _init_state():
        if has_cache:
            est[:, 0:POOL_HALO, :] = pool0_ref[...]
            for b in range(nb):
                kst[b, 0:BAND_PAST, :] = k0_ref[b].T.astype(BF16)
                vst[b, 0:BAND_PAST, :] = v0_ref[b].T.astype(BF16)
        else:
            est[:, 0:POOL_HALO, :] = jnp.zeros((nb, POOL_HALO, D_POOL), F32)
            kst[:, 0:BAND_PAST, :] = jnp.zeros((nb, BAND_PAST, D_ATTN), BF16)
            vst[:, 0:BAND_PAST, :] = jnp.zeros((nb, BAND_PAST, D_ATTN), BF16)
        if srows > BAND_PAST + tl:
            pad = srows - BAND_PAST - tl
            kst[:, BAND_PAST + tl:, :] = jnp.zeros((nb, pad, D_ATTN), BF16)
            vst[:, BAND_PAST + tl:, :] = jnp.zeros((nb, pad, D_ATTN), BF16)

    for src, dst in zip(cast_in, cast_out):
        dst[...] = src[...].astype(BF16)

    x = x_ref[...].reshape(m, D_MODEL)
    h = _rms(x, gmix_ref[...]).astype(BF16)
    h_sc[...] = h
    u = _dot(h, win_ref[:, 0:D_POOL])
    q = _dot(h, win_ref[:, D_POOL:D_POOL + D_ATTN]) * (HEAD_DIM ** -0.5 * LOG2E)
    k = _dot(h, win_ref[:, D_POOL + D_ATTN:D_POOL + 2 * D_ATTN])
    v = _dot(h, win_ref[:, D_POOL + 2 * D_ATTN:D_POOL + 3 * D_ATTN])

    lane = lax.broadcasted_iota(jnp.int32, (m, D_ATTN), 1)
    first_head = (lane % LANES) < HEAD_DIM
    qa_sc[...] = jnp.where(first_head, q, 0.0).astype(BF16)
    qb_sc[...] = jnp.where(first_head, 0.0, q).astype(BF16)

    kst[:, BAND_PAST:BAND_PAST + tl, :] = k.astype(BF16).reshape(nb, tl, D_ATTN)
    vst[:, BAND_PAST:BAND_PAST + tl, :] = v.astype(BF16).reshape(nb, tl, D_ATTN)

    if kv_slab is None:
        nk_ref[...] = k.reshape(nb, tl, D_ATTN)
        nv_ref[...] = v.reshape(nb, tl, D_ATTN)
    else:
        kv_sc[0][...] = k
        kv_sc[1][...] = v

    row = lax.broadcasted_iota(jnp.int32, (tl, POOL_GRP), 0)
    for b in range(nb):
        est[b, POOL_HALO:POOL_HALO + tl, :] = u[b * tl:(b + 1) * tl, :]
        e = est[b]
        for g, w in enumerate(POOL_WINDOWS):
            eg = e[:, g * POOL_GRP:(g + 1) * POOL_GRP]
            s = eg
            sh = 1
            while sh < w:
                s = s + pltpu.roll(s, sh, 0)
                sh *= 2
            if pos0 >= POOL_PAD:
                mean = s[POOL_HALO:, :] * (1.0 / w)
            else:
                cnt = jnp.minimum(pos0 + t * tl + row + 1, w).astype(F32)
                mean = s[POOL_HALO:, :] / cnt
            dg = mean - eg[POOL_HALO:, :]
            d_sc[b * tl:(b + 1) * tl, g * POOL_GRP:(g + 1) * POOL_GRP] = dg.astype(BF16)
        tail = e[tl:tl + POOL_HALO, :]
        est[b, 0:POOL_HALO, :] = tail
        npool_ref[b] = tail

    n_grp = tl // grp
    olane = lax.broadcasted_iota(jnp.int32, (grp, LANES), 1)
    for g in range(n_grp):
        n_early = min(kband, -(-max(0, BAND_PAST - pos0 - g * grp) // LANES) * LANES)
        if n_early:
            kcol = lax.broadcasted_iota(jnp.int32, (1, n_early), 1)
            first_pos = pos0 + t * tl + g * grp - BAND_PAST
            kmask = jnp.where(kcol + first_pos >= 0, 0.0, NEG_INF).astype(F32)
        for hp in range(N_HEADS // HEADS_PER_VREG):
            cols = slice(hp * LANES, (hp + 1) * LANES)
            ss = []
            for b in range(nb):
                r0 = b * tl + g * grp
                qs = jnp.concatenate([qa_sc[r0:r0 + grp, cols], qb_sc[r0:r0 + grp, cols]], axis=0)
                k2 = kst[b, g * grp:g * grp + kband, cols]
                ss.append(lax.dot_general(qs, k2, (((1,), (1,)), ((), ())), preferred_element_type=F32))
            tab2 = [tab_ref[hp * HEADS_PER_VREG + j] for j in range(HEADS_PER_VREG)]
            s = jnp.concatenate(ss, axis=0) + jnp.concatenate(tab2 * nb, axis=0)
            if n_early == kband:
                s = s + kmask
            elif n_early:
                s = jnp.concatenate([s[:, :n_early] + kmask, s[:, n_early:]], axis=1)
            p = jnp.exp2(s - jnp.max(s, axis=-1, keepdims=True))
            inv = 1.0 / jnp.sum(p, axis=-1, keepdims=True)
            p = p.astype(BF16)
            for b in range(nb):
                r0 = b * tl + g * grp
                rows = slice(b * 2 * grp, (b + 1) * 2 * grp)
                v2 = vst[b, g * grp:g * grp + kband, cols]
                o = _dot(p[rows, :], v2) * inv[rows, :]
                att_sc[r0:r0 + grp, cols] = jnp.where(olane < HEAD_DIM, o[0:grp, :], o[grp:, :]).astype(BF16)

    ys = [_dot(d_sc[:, g * POOL_GRP:(g + 1) * POOL_GRP], wgrp_ref[g]) for g in range(len(POOL_WINDOWS))]
    y = (jnp.concatenate(ys, axis=1) * pscale_ref[...]).astype(BF16)
    g_off = D_POOL + 3 * D_ATTN
    gate_p = jax.nn.sigmoid(_dot(h_sc[...], win_ref[:, g_off:g_off + D_MODEL]) + bgate_ref[:, 0:D_MODEL])
    merged = gate_p * _dot(y, wpp_ref[...])
    gate_a = jax.nn.sigmoid(_dot(h_sc[...], win_ref[:, g_off + D_MODEL:g_off + 2 * D_MODEL])
                            + bgate_ref[:, D_MODEL:2 * D_MODEL])
    merged = (merged + gate_a * _dot(att_sc[...], wap_ref[...])).astype(BF16)
    xo = x_ref[...].reshape(m, D_MODEL) + _dot(merged, wout_ref[...])
    xo_ref[...] = xo.reshape(nb, tl, D_MODEL)

    if nt > 1:
        kst[:, 0:BAND_PAST, :] = kst[:, tl:tl + BAND_PAST, :]
        vst[:, 0:BAND_PAST, :] = vst[:, tl:tl + BAND_PAST, :]

    if kv_slab is not None:
        @pl.when(t == nt - 1)
        def _emit_kv():
            for ref, sc in ((nk_ref, kv_sc[0]), (nv_ref, kv_sc[1])):
                kt = sc[...].T.reshape(nb, D_ATTN, tl)
                if n_prev:
                    ref[...] = kt
                else:
                    layer, depth = kv_slab
                    for l in range(depth):
                        ref[l] = kt if l == layer else jnp.zeros_like(kt)


def _key_band(grp):
    return -(-(BAND_PAST + grp) // LANES) * LANES


def _band_diag(rel_bias, grp):
    span = grp + _key_band(grp) - 1
    width = -(-span // LANES) * LANES
    rel = np.clip((grp - 1) - np.arange(span) + BAND_PAST, -REL_CLIP, REL_CLIP) + REL_CLIP
    diag = rel_bias[:, :, rel].astype(F32) * LOG2E
    return jnp.pad(diag, ((0, 0), (0, 0), (0, width - span)))


def _mixer(x, cache, lw, layer, prev_kv=None, stack_kv=False, cast=(), *, tl, grp, pos0):
    bsz, seq, _ = x.shape
    depth = lw[0][0].shape[0]
    has_cache = cache is not None
    nb = bsz if has_cache else 1
    nt = seq // tl
    assert seq % tl == 0 and tl % grp == 0 and tl % (2 * SUBLANES) == 0
    assert not (has_cache and nt != 1)
    assert not (stack_kv and nb != 1)
    kband = _key_band(grp)
    srows = max(BAND_PAST + tl, (tl // grp - 1) * grp + kband)
    m = nb * tl

    row_spec = pl.BlockSpec((nb, tl, D_MODEL), lambda b, t: (b, t, 0))
    in_specs = [row_spec]
    args = [x]
    if has_cache:
        kv0_spec = pl.BlockSpec((None, nb, D_ATTN, BAND_PAST), lambda b, t: (layer, b, 0, 0),
                                pipeline_mode=pl.Buffered(1))
        in_specs += [pl.BlockSpec((None, nb, POOL_HALO, D_POOL), lambda b, t: (layer, b, 0, 0)), kv0_spec, kv0_spec]
        args += list(cache)
    in_specs += [_weight_spec(w) for w in lw]
    args += [w[0] for w in lw]
    aliases = {}
    if prev_kv is not None:
        for n, arr in enumerate(prev_kv):
            aliases[len(args)] = 2 + n
            in_specs.append(pl.BlockSpec(memory_space=pl.ANY))
            args.append(arr)
    plans = [_cast_plan(w, layer, (bsz // nb) * nt, nt) for w in cast]
    in_specs += [plan[0] for plan in plans]
    args += list(cast)

    if not stack_kv:
        kv_shape = jax.ShapeDtypeStruct((bsz, tl, D_ATTN), F32)
        kv_spec = pl.BlockSpec((nb, tl, D_ATTN), lambda b, t: (b, 0, 0))
    else:
        kv_shape = jax.ShapeDtypeStruct((depth, bsz, D_ATTN, tl), F32)
        if prev_kv is None:
            kv_spec = pl.BlockSpec((depth, nb, D_ATTN, tl), lambda b, t: (0, b, 0, 0))
        else:
            kv_spec = pl.BlockSpec((None, nb, D_ATTN, tl), lambda b, t: (layer, b, 0, 0))
    out_shape = [jax.ShapeDtypeStruct((bsz, seq, D_MODEL), F32),
                 jax.ShapeDtypeStruct((bsz, POOL_HALO, D_POOL), F32), kv_shape, kv_shape]
    out_shape += [plan[2] for plan in plans]
    out_specs = [row_spec, pl.BlockSpec((nb, POOL_HALO, D_POOL), lambda b, t: (b, 0, 0)), kv_spec, kv_spec]
    out_specs += [plan[1] for plan in plans]
    scratch = [pltpu.VMEM((m, D_MODEL), BF16),
               pltpu.VMEM((nb, POOL_HALO + tl, D_POOL), F32),
               pltpu.VMEM((nb, srows, D_ATTN), BF16),
               pltpu.VMEM((nb, srows, D_ATTN), BF16),
               pltpu.VMEM((m, D_ATTN), BF16),
               pltpu.VMEM((m, D_ATTN), BF16),
               pltpu.VMEM((m, D_POOL), BF16),
               pltpu.VMEM((m, D_ATTN), BF16),
               pltpu.VMEM((N_HEADS, grp, kband), F32)]
    if stack_kv:
        scratch += [pltpu.VMEM((m, D_ATTN), F32)] * 2
    kern = functools.partial(_mixer_kernel, nb=nb, tl=tl, nt=nt, grp=grp, kband=kband,
                             pos0=pos0, has_cache=has_cache, kv_slab=(layer, depth) if stack_kv else None,
                             n_prev=len(aliases), n_cast=len(cast))
    return pl.pallas_call(
        kern,
        grid=(bsz // nb, nt),
        in_specs=in_specs,
        out_specs=out_specs,
        out_shape=out_shape,
        scratch_shapes=scratch,
        input_output_aliases=aliases,
        compiler_params=pltpu.CompilerParams(
            dimension_semantics=("arbitrary", "arbitrary"), vmem_limit_bytes=VMEM_LIMIT_BYTES),
        name="mixer_cached" if has_cache else "mixer",
    )(*args)


def _ffn_kernel(x_ref, p_ref, conv0_ref, gffn_ref, wup_ref, wdw_ref, bdw_ref, wdown_ref,
                gple_ref, wple_ref, wpg_ref, gfin_ref, *rest, nb, tl, n_cast, final_norm):
    cast_in = rest[:n_cast]
    xo_ref, nconv_ref = rest[n_cast:n_cast + 2]
    cast_out = rest[n_cast + 2:2 * n_cast + 2]
    cst, h_sc, y_sc = rest[2 * n_cast + 2:]
    m = nb * tl
    t = pl.program_id(1)

    @pl.when(t == 0)
    def _init_state():
        cst[...] = conv0_ref[...]

    for src, dst in zip(cast_in, cast_out):
        dst[...] = src[...].astype(BF16)

    x = x_ref[...].reshape(m, D_MODEL)
    h_sc[...] = _rms(x, gffn_ref[...]).astype(BF16)
    row = lax.broadcasted_iota(jnp.int32, (tl, FF_CHUNK), 0)
    for j in range(N_FF_CHUNKS):
        cols = slice(j * FF_CHUNK, (j + 1) * FF_CHUNK)
        a_all = _dot(h_sc[...], wup_ref[:, cols])
        g_all = _dot(h_sc[...], wup_ref[:, D_FF + j * FF_CHUNK:D_FF + (j + 1) * FF_CHUNK])
        w0, w1, w2 = wdw_ref[0:1, cols], wdw_ref[1:2, cols], wdw_ref[2:3, cols]
        bias = bdw_ref[:, cols]
        for b in range(nb):
            a = a_all[b * tl:(b + 1) * tl, :]
            gate = g_all[b * tl:(b + 1) * tl, :]
            c6 = cst[b, CONV_HALO - 2:CONV_HALO - 1, cols]
            c7 = cst[b, CONV_HALO - 1:CONV_HALO, cols]
            a1 = jnp.where(row == 0, c7, pltpu.roll(a, 1, 0))
            a2 = jnp.where(row == 0, c6, jnp.where(row == 1, c7, pltpu.roll(a, 2, 0)))
            conv = bias + a2 * w0 + a1 * w1 + a * w2
            y_sc[b * tl:(b + 1) * tl, cols] = (jax.nn.gelu(conv) * gate).astype(BF16)
            tail = a[tl - CONV_HALO:tl, :]
            cst[b, :, cols] = tail
            nconv_ref[b, :, cols] = tail

    x2 = x_ref[...].reshape(m, D_MODEL) + _dot(y_sc[...], wdown_ref[...])
    hp = _rms(x2, gple_ref[...]).astype(BF16)
    gate = jax.nn.sigmoid(_dot(hp, wpg_ref[...]))
    pe = _dot(p_ref[...].reshape(m, PLE_DIM).astype(BF16), wple_ref[...])
    x3 = x2 + pe * gate
    if final_norm:
        x3 = _rms(x3, gfin_ref[...])
    xo_ref[...] = x3.reshape(nb, tl, D_MODEL)


def _ffn(x, p, conv0, lw, layer, cast=(), *, tl, nb, final_norm):
    bsz, seq, _ = x.shape
    nt = seq // tl
    assert seq % tl == 0 and bsz % nb == 0 and tl % CONV_HALO == 0
    m = nb * tl
    row_spec = pl.BlockSpec((nb, tl, D_MODEL), lambda b, t: (b, t, 0))
    in_specs = [row_spec,
                pl.BlockSpec((None, nb, tl, PLE_DIM), lambda b, t: (layer, b, t, 0)),
                pl.BlockSpec((None, nb, CONV_HALO, D_FF), lambda b, t: (layer, b, 0, 0))]
    in_specs += [_weight_spec(w) for w in lw]
    plans = [_cast_plan(w, layer + 1, (bsz // nb) * nt, nt) for w in cast]
    in_specs += [plan[0] for plan in plans]
    kern = functools.partial(_ffn_kernel, nb=nb, tl=tl, n_cast=len(cast), final_norm=final_norm)
    return pl.pallas_call(
        kern,
        grid=(bsz // nb, nt),
        in_specs=in_specs,
        out_specs=([row_spec, pl.BlockSpec((nb, CONV_HALO, D_FF), lambda b, t: (b, 0, 0))]
                   + [plan[1] for plan in plans]),
        out_shape=([jax.ShapeDtypeStruct((bsz, seq, D_MODEL), F32),
                    jax.ShapeDtypeStruct((bsz, CONV_HALO, D_FF), F32)] + [plan[2] for plan in plans]),
        scratch_shapes=[pltpu.VMEM((nb, CONV_HALO, D_FF), F32),
                        pltpu.VMEM((m, D_MODEL), BF16),
                        pltpu.VMEM((m, D_FF), BF16)],
        compiler_params=pltpu.CompilerParams(
            dimension_semantics=("arbitrary", "arbitrary"), vmem_limit_bytes=VMEM_LIMIT_BYTES),
        name="ffn",
    )(x, p, conv0, *[w[0] for w in lw], *cast)


PROMPT_TILE = 512
PROMPT_FFN_TILE = 1024
PROMPT_GROUP = 256


def kernel(x_prompt, x_sample, cache_pool, cache_k, cache_v, cache_ffn_conv, p_prompt, p_sample,
           g_mix, w_in, b_gate, w_pool_grp, pool_scale, rel_bias, w_pool_proj, w_attn_proj, w_out,
           g_ffn, w_up, w_dw, b_dw, w_down, g_ple, w_ple, w_ple_gate, g_final):
    depth = w_in.shape[0]
    bp = x_prompt.shape[0]
    bs, ls, _ = x_sample.shape

    bf = lambda a: a.astype(BF16)
    row2 = lambda a: a[:, None, :]
    diag_p = _band_diag(rel_bias, PROMPT_GROUP)
    diag_s = _band_diag(rel_bias, ls)
    small = [row2(g_mix), row2(b_gate), bf(w_pool_grp), row2(pool_scale), bf(w_pool_proj), bf(w_attn_proj),
             bf(w_out)]
    g_ffn2, b_dw2, g_ple2, wple_b, wpg_b = row2(g_ffn), row2(b_dw), row2(g_ple), bf(w_ple), bf(w_ple_gate)
    g_fin2 = jnp.broadcast_to(g_final[None, None, :], (depth, 1, D_MODEL))

    def mixer_weights(i, win_b, diag):
        g_mix2, b_gate2, wgrp_b, pscale2, wpp_b, wap_b, wout_b = small
        return [(g_mix2, i), (win_b, None), (b_gate2, i), (wgrp_b, i), (pscale2, i), (wpp_b, i), (wap_b, i),
                (wout_b, i), (diag, i)]

    def ffn_weights(i, wup_b, wdown_b):
        return [(g_ffn2, i), (wup_b, None), (w_dw, i), (b_dw2, i), (wdown_b, None), (g_ple2, i), (wple_b, i),
                (wpg_b, i), (g_fin2, i)]

    pool0_s = jnp.pad(cache_pool, ((0, 0), (0, 0), (POOL_HALO - POOL_PAD, 0), (0, 0)))
    k0_s = cache_k.transpose(0, 1, 3, 4, 2).reshape(depth, bs, D_ATTN, BAND_PAST)
    v0_s = cache_v.transpose(0, 1, 3, 4, 2).reshape(depth, bs, D_ATTN, BAND_PAST)
    conv0_s = jnp.pad(cache_ffn_conv, ((0, 0), (0, 0), (CONV_HALO - cache_ffn_conv.shape[2], 0), (0, 0)))
    conv0_p = jnp.zeros((depth, bp, CONV_HALO, D_FF), F32)

    xp, xs = x_prompt, x_sample
    outs_p, outs_s = [], []
    kv_p = None
    win_b, ffn_b = bf(w_in[0]), [bf(w_up[0]), bf(w_down[0])]
    for i in range(depth):
        fin = i == depth - 1
        xp, npool, nk, nv, *cast_b = _mixer(xp, None, mixer_weights(i, win_b, diag_p), i, kv_p, True,
                                            [w_up, w_down] if i else [], tl=PROMPT_TILE, grp=PROMPT_GROUP, pos0=0)
        wup_b, wdown_b = cast_b or ffn_b
        kv_p = (nk, nv)
        xp, nconv, *next_win = _ffn(xp, p_prompt, conv0_p, ffn_weights(i, wup_b, wdown_b), i, [] if fin else [w_in],
                                    tl=PROMPT_FFN_TILE, nb=1, final_norm=fin)
        outs_p.append((npool, nconv))
        xs, npool, nk, nv = _mixer(xs, (pool0_s, k0_s, v0_s), mixer_weights(i, win_b, diag_s), i,
                                   tl=ls, grp=ls, pos0=PAST_LEN)
        xs, nconv = _ffn(xs, p_sample, conv0_s, ffn_weights(i, wup_b, wdown_b), i, tl=ls, nb=bs, final_norm=fin)
        outs_s.append((npool, nconv, nk, nv))
        if next_win:
            win_b = next_win[0]

    def pool_conv(outs):
        return (jnp.stack([o[0][:, POOL_HALO - POOL_PAD:, :] for o in outs]),
                jnp.stack([o[1][:, CONV_HALO - 2:, :] for o in outs]))

    def by_head(kv_t):
        return kv_t.reshape(depth, bp, N_HEADS, HEAD_DIM, -1).transpose(0, 1, 4, 2, 3)

    pp, cp = pool_conv(outs_p)
    ps, cs = pool_conv(outs_s)
    ks = jnp.stack([o[2].reshape(bs, -1, N_HEADS, HEAD_DIM) for o in outs_s])
    vs = jnp.stack([o[3].reshape(bs, -1, N_HEADS, HEAD_DIM) for o in outs_s])
    return (xp, xs, pp, by_head(kv_p[0]), by_head(kv_p[1]), cp, ps, ks, vs, cs)
```

```python
import functools
import math

import jax
import jax.numpy as jnp
import numpy as np
from jax import lax
from jax.experimental import pallas as pl
from jax.experimental.pallas import tpu as pltpu

D_MODEL = 1024
CHUNK = 64
BAND_PAST = 512
D_ATTN = 512
PAST_LEN = 2048
HEAD_DIM = 64
N_HEADS = 8
REL_CLIP = 128
D_POOL = 512
POOL_WINDOWS = (2, 4, 8, 16)
POOL_GRP = 128
POOL_PAD = 15
D_FF = 2816
PLE_DIM = 256
EPS = 1e-6
NEG_INF = -1e30
LOG2E = math.log2(math.e)

LANES = 128
SUBLANES = 8
MXU_DIM = 256
VMEM_LIMIT_BYTES = 56 * 1024 * 1024

POOL_HALO = 16
CONV_HALO = SUBLANES
FF_CHUNK = MXU_DIM
N_FF_CHUNKS = D_FF // FF_CHUNK
HEADS_PER_VREG = LANES // HEAD_DIM

BF16 = jnp.bfloat16
F32 = jnp.float32


def _rms(x, g):
    return x * lax.rsqrt(jnp.mean(x * x, axis=-1, keepdims=True) + EPS) * g


def _dot(a, b):
    return jnp.dot(a, b, preferred_element_type=F32)


def _layer_spec(arr, layer):
    shape = arr.shape[1:]
    nd = len(shape)
    return pl.BlockSpec((None,) + shape, lambda b, t: (layer,) + (0,) * nd, pipeline_mode=pl.Buffered(1))


def _mixer_kernel(*refs, nb, tl, nt, grp, kband, pos0, has_cache, kv_slab, n_prev):
    it = iter(refs)
    x_ref = next(it)
    if has_cache:
        pool0_ref, k0_ref, v0_ref = next(it), next(it), next(it)
    gmix_ref, win_ref, bgate_ref, wgrp_ref, pscale_ref, wpp_ref, wap_ref, wout_ref, diag_ref = (
        next(it) for _ in range(9))
    for _ in range(n_prev):
        next(it)
    (xo_ref, npool_ref, nk_ref, nv_ref,
     h_sc, est, kst, vst, qa_sc, qb_sc, d_sc, att_sc, tab_ref) = it

    m = nb * tl
    t = pl.program_id(1)
    srows = kst.shape[1]

    @pl.when((pl.program_id(0) == 0) & (t == 0))
    def _build_band_table():
        width = diag_ref.shape[1]
        i = lax.broadcasted_iota(jnp.int32, (grp, kband), 0)
        j = lax.broadcasted_iota(jnp.int32, (grp, kband), 1)
        lo = jnp.bitwise_and(i, -CHUNK)
        valid = (j >= lo) & (j < lo + BAND_PAST + CHUNK) & (j < BAND_PAST + grp)
        for hd in range(N_HEADS):
            d = jnp.broadcast_to(diag_ref[hd:hd + 1, :], (grp, width))
            shifted = pltpu.roll(d, width - (grp - 1), 1, stride=1, stride_axis=0)
            tab_ref[hd] = jnp.where(valid, shifted[:, 0:kband], NEG_INF)

    @pl.when(t == 0)
    def _init_state():
        if has_cache:
            est[:, 0:POOL_HALO, :] = pool0_ref[...]
            for b in range(nb):
                kst[b, 0:BAND_PAST, :] = k0_ref[b].T.astype(BF16)
                vst[b, 0:BAND_PAST, :] = v0_ref[b].T.astype(BF16)
        else:
            est[:, 0:POOL_HALO, :] = jnp.zeros((nb, POOL_HALO, D_POOL), F32)
            kst[:, 0:BAND_PAST, :] = jnp.zeros((nb, BAND_PAST, D_ATTN), BF16)
            vst[:, 0:BAND_PAST, :] = jnp.zeros((nb, BAND_PAST, D_ATTN), BF16)
        if kv_slab is not None and not n_prev:
            layer, depth = kv_slab
            for other in [l for l in range(depth) if l != layer]:
                nk_ref[other] = jnp.zeros((nb, D_ATTN, tl), F32)
                nv_ref[other] = jnp.zeros((nb, D_ATTN, tl), F32)
        if srows > BAND_PAST + tl:
            pad = srows - BAND_PAST - tl
            kst[:, BAND_PAST + tl:, :] = jnp.zeros((nb, pad, D_ATTN), BF16)
            vst[:, BAND_PAST + tl:, :] = jnp.zeros((nb, pad, D_ATTN), BF16)

    x = x_ref[...].reshape(m, D_MODEL)
    h = _rms(x, gmix_ref[...]).astype(BF16)
    h_sc[...] = h
    u = _dot(h, win_ref[:, 0:D_POOL])
    q = _dot(h, win_ref[:, D_POOL:D_POOL + D_ATTN]) * (HEAD_DIM ** -0.5 * LOG2E)
    k = _dot(h, win_ref[:, D_POOL + D_ATTN:D_POOL + 2 * D_ATTN])
    v = _dot(h, win_ref[:, D_POOL + 2 * D_ATTN:D_POOL + 3 * D_ATTN])

    lane = lax.broadcasted_iota(jnp.int32, (m, D_ATTN), 1)
    first_head = (lane % LANES) < HEAD_DIM
    qa_sc[...] = jnp.where(first_head, q, 0.0).astype(BF16)
    qb_sc[...] = jnp.where(first_head, 0.0, q).astype(BF16)

    kst[:, BAND_PAST:BAND_PAST + tl, :] = k.astype(BF16).reshape(nb, tl, D_ATTN)
    vst[:, BAND_PAST:BAND_PAST + tl, :] = v.astype(BF16).reshape(nb, tl, D_ATTN)

    if kv_slab is None:
        nk_ref[...] = k.reshape(nb, tl, D_ATTN)
        nv_ref[...] = v.reshape(nb, tl, D_ATTN)
    else:
        for ref, new in ((nk_ref, k), (nv_ref, v)):
            new_t = new.T.reshape(nb, D_ATTN, tl)
            if n_prev:
                ref[...] = new_t
            else:
                ref[kv_slab[0]] = new_t

    row = lax.broadcasted_iota(jnp.int32, (tl, POOL_GRP), 0)
    for b in range(nb):
        est[b, POOL_HALO:POOL_HALO + tl, :] = u[b * tl:(b + 1) * tl, :]
        e = est[b]
        for g, w in enumerate(POOL_WINDOWS):
            eg = e[:, g * POOL_GRP:(g + 1) * POOL_GRP]
            s = eg
            sh = 1
            while sh < w:
                s = s + pltpu.roll(s, sh, 0)
                sh *= 2
            if pos0 >= POOL_PAD:
                mean = s[POOL_HALO:, :] * (1.0 / w)
            else:
                cnt = jnp.minimum(pos0 + t * tl + row + 1, w).astype(F32)
                mean = s[POOL_HALO:, :] / cnt
            dg = mean - eg[POOL_HALO:, :]
            d_sc[b * tl:(b + 1) * tl, g * POOL_GRP:(g + 1) * POOL_GRP] = dg.astype(BF16)
        tail = e[tl:tl + POOL_HALO, :]
        est[b, 0:POOL_HALO, :] = tail
        npool_ref[b] = tail

    n_grp = tl // grp
    olane = lax.broadcasted_iota(jnp.int32, (grp, LANES), 1)
    for g in range(n_grp):
        n_early = min(kband, -(-max(0, BAND_PAST - pos0 - g * grp) // LANES) * LANES)
        if n_early:
            kcol = lax.broadcasted_iota(jnp.int32, (1, n_early), 1)
            first_pos = pos0 + t * tl + g * grp - BAND_PAST
            kmask = jnp.where(kcol + first_pos >= 0, 0.0, NEG_INF).astype(F32)
        for hp in range(N_HEADS // HEADS_PER_VREG):
            cols = slice(hp * LANES, (hp + 1) * LANES)
            ss = []
            for b in range(nb):
                r0 = b * tl + g * grp
                qs = jnp.concatenate([qa_sc[r0:r0 + grp, cols], qb_sc[r0:r0 + grp, cols]], axis=0)
                k2 = kst[b, g * grp:g * grp + kband, cols]
                ss.append(lax.dot_general(qs, k2, (((1,), (1,)), ((), ())), preferred_element_type=F32))
            tab2 = [tab_ref[hp * HEADS_PER_VREG + j] for j in range(HEADS_PER_VREG)]
            s = jnp.concatenate(ss, axis=0) + jnp.concatenate(tab2 * nb, axis=0)
            if n_early == kband:
                s = s + kmask
            elif n_early:
                s = jnp.concatenate([s[:, :n_early] + kmask, s[:, n_early:]], axis=1)
            p = jnp.exp2(s - jnp.max(s, axis=-1, keepdims=True))
            inv = 1.0 / jnp.sum(p, axis=-1, keepdims=True)
            p = p.astype(BF16)
            for b in range(nb):
                r0 = b * tl + g * grp
                rows = slice(b * 2 * grp, (b + 1) * 2 * grp)
                v2 = vst[b, g * grp:g * grp + kband, cols]
                o = _dot(p[rows, :], v2) * inv[rows, :]
                att_sc[r0:r0 + grp, cols] = jnp.where(olane < HEAD_DIM, o[0:grp, :], o[grp:, :]).astype(BF16)

    ys = [_dot(d_sc[:, g * POOL_GRP:(g + 1) * POOL_GRP], wgrp_ref[g]) for g in range(len(POOL_WINDOWS))]
    y = (jnp.concatenate(ys, axis=1) * pscale_ref[...]).astype(BF16)
    g_off = D_POOL + 3 * D_ATTN
    gate_p = jax.nn.sigmoid(_dot(h_sc[...], win_ref[:, g_off:g_off + D_MODEL]) + bgate_ref[:, 0:D_MODEL])
    merged = gate_p * _dot(y, wpp_ref[...])
    gate_a = jax.nn.sigmoid(_dot(h_sc[...], win_ref[:, g_off + D_MODEL:g_off + 2 * D_MODEL])
                            + bgate_ref[:, D_MODEL:2 * D_MODEL])
    merged = (merged + gate_a * _dot(att_sc[...], wap_ref[...])).astype(BF16)
    xo = x_ref[...].reshape(m, D_MODEL) + _dot(merged, wout_ref[...])
    xo_ref[...] = xo.reshape(nb, tl, D_MODEL)

    if nt > 1:
        kst[:, 0:BAND_PAST, :] = kst[:, tl:tl + BAND_PAST, :]
        vst[:, 0:BAND_PAST, :] = vst[:, tl:tl + BAND_PAST, :]


def _key_band(grp):
    return -(-(BAND_PAST + grp) // LANES) * LANES


def _band_diag(rel_bias, grp):
    span = grp + _key_band(grp) - 1
    width = -(-span // LANES) * LANES
    rel = np.clip((grp - 1) - np.arange(span) + BAND_PAST, -REL_CLIP, REL_CLIP) + REL_CLIP
    diag = rel_bias[:, :, rel].astype(F32) * LOG2E
    return jnp.pad(diag, ((0, 0), (0, 0), (0, width - span)))


def _mixer(x, cache, lw, layer, prev_kv=None, stack_kv=False, *, tl, grp, pos0):
    bsz, seq, _ = x.shape
    depth = lw[0].shape[0]
    has_cache = cache is not None
    nb = bsz if has_cache else 1
    nt = seq // tl
    assert seq % tl == 0 and tl % grp == 0 and tl % (2 * SUBLANES) == 0
    assert not (has_cache and nt != 1)
    assert not (stack_kv and nb != 1)
    kband = _key_band(grp)
    srows = max(BAND_PAST + tl, (tl // grp - 1) * grp + kband)
    m = nb * tl

    row_spec = pl.BlockSpec((nb, tl, D_MODEL), lambda b, t: (b, t, 0))
    in_specs = [row_spec]
    args = [x]
    if has_cache:
        kv0_spec = pl.BlockSpec((None, nb, D_ATTN, BAND_PAST), lambda b, t: (layer, b, 0, 0),
                                pipeline_mode=pl.Buffered(1))
        in_specs += [pl.BlockSpec((None, nb, POOL_HALO, D_POOL), lambda b, t: (layer, b, 0, 0)), kv0_spec, kv0_spec]
        args += list(cache)
    in_specs += [_layer_spec(w, layer) for w in lw]
    args += list(lw)
    aliases = {}
    if prev_kv is not None:
        for n, arr in enumerate(prev_kv):
            aliases[len(args)] = 2 + n
            in_specs.append(pl.BlockSpec(memory_space=pl.ANY))
            args.append(arr)

    if not stack_kv:
        kv_shape = jax.ShapeDtypeStruct((bsz, tl, D_ATTN), F32)
        kv_spec = pl.BlockSpec((nb, tl, D_ATTN), lambda b, t: (b, 0, 0))
    else:
        kv_shape = jax.ShapeDtypeStruct((depth, bsz, D_ATTN, tl), F32)
        if prev_kv is None:
            kv_spec = pl.BlockSpec((depth, nb, D_ATTN, tl), lambda b, t: (0, b, 0, 0))
        else:
            kv_spec = pl.BlockSpec((None, nb, D_ATTN, tl), lambda b, t: (layer, b, 0, 0))
    out_shape = [jax.ShapeDtypeStruct((bsz, seq, D_MODEL), F32),
                 jax.ShapeDtypeStruct((bsz, POOL_HALO, D_POOL), F32), kv_shape, kv_shape]
    out_specs = [row_spec, pl.BlockSpec((nb, POOL_HALO, D_POOL), lambda b, t: (b, 0, 0)), kv_spec, kv_spec]
    scratch = [pltpu.VMEM((m, D_MODEL), BF16),
               pltpu.VMEM((nb, POOL_HALO + tl, D_POOL), F32),
               pltpu.VMEM((nb, srows, D_ATTN), BF16),
               pltpu.VMEM((nb, srows, D_ATTN), BF16),
               pltpu.VMEM((m, D_ATTN), BF16),
               pltpu.VMEM((m, D_ATTN), BF16),
               pltpu.VMEM((m, D_POOL), BF16),
               pltpu.VMEM((m, D_ATTN), BF16),
               pltpu.VMEM((N_HEADS, grp, kband), F32)]
    kern = functools.partial(_mixer_kernel, nb=nb, tl=tl, nt=nt, grp=grp, kband=kband,
                             pos0=pos0, has_cache=has_cache, kv_slab=(layer, depth) if stack_kv else None,
                             n_prev=len(aliases))
    return pl.pallas_call(
        kern,
        grid=(bsz // nb, nt),
        in_specs=in_specs,
        out_specs=out_specs,
        out_shape=out_shape,
        scratch_shapes=scratch,
        input_output_aliases=aliases,
        compiler_params=pltpu.CompilerParams(
            dimension_semantics=("arbitrary", "arbitrary"), vmem_limit_bytes=VMEM_LIMIT_BYTES),
        name="mixer_cached" if has_cache else "mixer",
    )(*args)


def _ffn_kernel(x_ref, p_ref, conv0_ref, gffn_ref, wup_ref, wdw_ref, bdw_ref, wdown_ref,
                gple_ref, wple_ref, wpg_ref, gfin_ref,
                xo_ref, nconv_ref, cst, h_sc, y_sc, *, nb, tl, final_norm):
    m = nb * tl
    t = pl.program_id(1)

    @pl.when(t == 0)
    def _init_state():
        cst[...] = conv0_ref[...]

    x = x_ref[...].reshape(m, D_MODEL)
    h_sc[...] = _rms(x, gffn_ref[...]).astype(BF16)
    row = lax.broadcasted_iota(jnp.int32, (tl, FF_CHUNK), 0)
    for j in range(N_FF_CHUNKS):
        cols = slice(j * FF_CHUNK, (j + 1) * FF_CHUNK)
        a_all = _dot(h_sc[...], wup_ref[:, cols])
        g_all = _dot(h_sc[...], wup_ref[:, D_FF + j * FF_CHUNK:D_FF + (j + 1) * FF_CHUNK])
        w0, w1, w2 = wdw_ref[0:1, cols], wdw_ref[1:2, cols], wdw_ref[2:3, cols]
        bias = bdw_ref[:, cols]
        for b in range(nb):
            a = a_all[b * tl:(b + 1) * tl, :]
            gate = g_all[b * tl:(b + 1) * tl, :]
            c6 = cst[b, CONV_HALO - 2:CONV_HALO - 1, cols]
            c7 = cst[b, CONV_HALO - 1:CONV_HALO, cols]
            a1 = jnp.where(row == 0, c7, pltpu.roll(a, 1, 0))
            a2 = jnp.where(row == 0, c6, jnp.where(row == 1, c7, pltpu.roll(a, 2, 0)))
            conv = bias + a2 * w0 + a1 * w1 + a * w2
            y_sc[b * tl:(b + 1) * tl, cols] = (jax.nn.gelu(conv) * gate).astype(BF16)
            tail = a[tl - CONV_HALO:tl, :]
            cst[b, :, cols] = tail
            nconv_ref[b, :, cols] = tail

    x2 = x_ref[...].reshape(m, D_MODEL) + _dot(y_sc[...], wdown_ref[...])
    hp = _rms(x2, gple_ref[...]).astype(BF16)
    gate = jax.nn.sigmoid(_dot(hp, wpg_ref[...]))
    pe = _dot(p_ref[...].reshape(m, PLE_DIM).astype(BF16), wple_ref[...])
    x3 = x2 + pe * gate
    if final_norm:
        x3 = _rms(x3, gfin_ref[...])
    xo_ref[...] = x3.reshape(nb, tl, D_MODEL)


def _ffn(x, p, conv0, lw, layer, *, tl, nb, final_norm):
    bsz, seq, _ = x.shape
    nt = seq // tl
    assert seq % tl == 0 and bsz % nb == 0 and tl % CONV_HALO == 0
    m = nb * tl
    weights = list(lw)
    row_spec = pl.BlockSpec((nb, tl, D_MODEL), lambda b, t: (b, t, 0))
    in_specs = [row_spec,
                pl.BlockSpec((None, nb, tl, PLE_DIM), lambda b, t: (layer, b, t, 0)),
                pl.BlockSpec((None, nb, CONV_HALO, D_FF), lambda b, t: (layer, b, 0, 0))]
    in_specs += [_layer_spec(w, layer) for w in weights]
    kern = functools.partial(_ffn_kernel, nb=nb, tl=tl, final_norm=final_norm)
    return pl.pallas_call(
        kern,
        grid=(bsz // nb, nt),
        in_specs=in_specs,
        out_specs=[row_spec, pl.BlockSpec((nb, CONV_HALO, D_FF), lambda b, t: (b, 0, 0))],
        out_shape=[jax.ShapeDtypeStruct((bsz, seq, D_MODEL), F32),
                   jax.ShapeDtypeStruct((bsz, CONV_HALO, D_FF), F32)],
        scratch_shapes=[pltpu.VMEM((nb, CONV_HALO, D_FF), F32),
                        pltpu.VMEM((m, D_MODEL), BF16),
                        pltpu.VMEM((m, D_FF), BF16)],
        compiler_params=pltpu.CompilerParams(
            dimension_semantics=("arbitrary", "arbitrary"), vmem_limit_bytes=VMEM_LIMIT_BYTES),
        name="ffn",
    )(x, p, conv0, *weights)


PROMPT_TILE = 512
PROMPT_FFN_TILE = 1024
PROMPT_GROUP = 256


def kernel(x_prompt, x_sample, cache_pool, cache_k, cache_v, cache_ffn_conv, p_prompt, p_sample,
           g_mix, w_in, b_gate, w_pool_grp, pool_scale, rel_bias, w_pool_proj, w_attn_proj, w_out,
           g_ffn, w_up, w_dw, b_dw, w_down, g_ple, w_ple, w_ple_gate, g_final):
    depth = w_in.shape[0]
    bp = x_prompt.shape[0]
    bs, ls, _ = x_sample.shape

    bf = lambda a: a.astype(BF16)
    row2 = lambda a: a[:, None, :]
    mix_w = [row2(g_mix), bf(w_in), row2(b_gate), bf(w_pool_grp), row2(pool_scale), bf(w_pool_proj),
             bf(w_attn_proj), bf(w_out)]
    ffn_w = [row2(g_ffn), bf(w_up), w_dw, row2(b_dw), bf(w_down), row2(g_ple), bf(w_ple), bf(w_ple_gate),
             jnp.broadcast_to(g_final[None, None, :], (depth, 1, D_MODEL))]
    diag_p = _band_diag(rel_bias, PROMPT_GROUP)
    diag_s = _band_diag(rel_bias, ls)

    pool0_s = jnp.pad(cache_pool, ((0, 0), (0, 0), (POOL_HALO - POOL_PAD, 0), (0, 0)))
    k0_s = cache_k.transpose(0, 1, 3, 4, 2).reshape(depth, bs, D_ATTN, BAND_PAST)
    v0_s = cache_v.transpose(0, 1, 3, 4, 2).reshape(depth, bs, D_ATTN, BAND_PAST)
    conv0_s = jnp.pad(cache_ffn_conv, ((0, 0), (0, 0), (CONV_HALO - cache_ffn_conv.shape[2], 0), (0, 0)))
    conv0_p = jnp.zeros((depth, bp, CONV_HALO, D_FF), F32)

    xp, xs = x_prompt, x_sample
    outs_p, outs_s = [], []
    kv_p = None
    for i in range(depth):
        fin = i == depth - 1
        xp, npool, *kv_p = _mixer(xp, None, mix_w + [diag_p], i, kv_p, True,
                                  tl=PROMPT_TILE, grp=PROMPT_GROUP, pos0=0)
        xp, nconv = _ffn(xp, p_prompt, conv0_p, ffn_w, i, tl=PROMPT_FFN_TILE, nb=1, final_norm=fin)
        outs_p.append((npool, nconv))
        xs, npool, nk, nv = _mixer(xs, (pool0_s, k0_s, v0_s), mix_w + [diag_s], i, tl=ls, grp=ls, pos0=PAST_LEN)
        xs, nconv = _ffn(xs, p_sample, conv0_s, ffn_w, i, tl=ls, nb=bs, final_norm=fin)
        outs_s.append((npool, nconv, nk, nv))

    def pool_conv(outs):
        return (jnp.stack([o[0][:, POOL_HALO - POOL_PAD:, :] for o in outs]),
                jnp.stack([o[1][:, CONV_HALO - 2:, :] for o in outs]))

    def by_head(kv_t):
        return kv_t.reshape(depth, bp, N_HEADS, HEAD_DIM, -1).transpose(0, 1, 4, 2, 3)

    pp, cp = pool_conv(outs_p)
    ps, cs = pool_conv(outs_s)
    ks = jnp.stack([o[2].reshape(bs, -1, N_HEADS, HEAD_DIM) for o in outs_s])
    vs = jnp.stack([o[3].reshape(bs, -1, N_HEADS, HEAD_DIM) for o in outs_s])
    return (xp, xs, pp, by_head(kv_p[0]), by_head(kv_p[1]), cp, ps, ks, vs, cs)
```

```python
import functools
import math

import jax
import jax.numpy as jnp
import numpy as np
from jax import lax
from jax.experimental import pallas as pl
from jax.experimental.pallas import tpu as pltpu

D_MODEL = 1024
CHUNK = 64
BAND_PAST = 512
D_ATTN = 512
PAST_LEN = 2048
HEAD_DIM = 64
N_HEADS = 8
REL_CLIP = 128
D_POOL = 512
POOL_WINDOWS = (2, 4, 8, 16)
POOL_GRP = 128
POOL_PAD = 15
D_FF = 2816
PLE_DIM = 256
EPS = 1e-6
NEG_INF = -1e30
LOG2E = math.log2(math.e)

LANES = 128
SUBLANES = 8
MXU_DIM = 256
VMEM_LIMIT_BYTES = 56 * 1024 * 1024

POOL_HALO = 16
CONV_HALO = SUBLANES
FF_CHUNK = MXU_DIM
N_FF_CHUNKS = D_FF // FF_CHUNK
HEADS_PER_VREG = LANES // HEAD_DIM

BF16 = jnp.bfloat16
F32 = jnp.float32


def _rms(x, g):
    return x * lax.rsqrt(jnp.mean(x * x, axis=-1, keepdims=True) + EPS) * g


def _dot(a, b):
    return jnp.dot(a, b, preferred_element_type=F32)


def _layer_spec(arr, layer):
    shape = arr.shape[1:]
    nd = len(shape)
    return pl.BlockSpec((None,) + shape, lambda b, t: (layer,) + (0,) * nd, pipeline_mode=pl.Buffered(1))


def _mixer_kernel(*refs, nb, tl, nt, grp, kband, pos0, has_cache, kv_slab, n_prev):
    it = iter(refs)
    x_ref = next(it)
    if has_cache:
        pool0_ref, k0_ref, v0_ref = next(it), next(it), next(it)
    gmix_ref, win_ref, bgate_ref, wgrp_ref, pscale_ref, wpp_ref, wap_ref, wout_ref, diag_ref = (
        next(it) for _ in range(9))
    for _ in range(n_prev):
        next(it)
    (xo_ref, npool_ref, nk_ref, nv_ref,
     h_sc, est, kst, vst, qa_sc, qb_sc, d_sc, att_sc, tab_ref) = it

    m = nb * tl
    t = pl.program_id(1)
    srows = kst.shape[1]

    @pl.when((pl.program_id(0) == 0) & (t == 0))
    def _build_band_table():
        width = diag_ref.shape[1]
        i = lax.broadcasted_iota(jnp.int32, (grp, kband), 0)
        j = lax.broadcasted_iota(jnp.int32, (grp, kband), 1)
        lo = jnp.bitwise_and(i, -CHUNK)
        valid = (j >= lo) & (j < lo + BAND_PAST + CHUNK) & (j < BAND_PAST + grp)
        for hd in range(N_HEADS):
            d = jnp.broadcast_to(diag_ref[hd:hd + 1, :], (grp, width))
            shifted = pltpu.roll(d, width - (grp - 1), 1, stride=1, stride_axis=0)
            tab_ref[hd] = jnp.where(valid, shifted[:, 0:kband], NEG_INF)

    @pl.when(t == 0)
    def _init_state():
        if has_cache:
            est[:, 0:POOL_HALO, :] = pool0_ref[...]
            for b in range(nb):
                kst[b, 0:BAND_PAST, :] = k0_ref[b].T.astype(BF16)
                vst[b, 0:BAND_PAST, :] = v0_ref[b].T.astype(BF16)
        else:
            est[:, 0:POOL_HALO, :] = jnp.zeros((nb, POOL_HALO, D_POOL), F32)
            kst[:, 0:BAND_PAST, :] = jnp.zeros((nb, BAND_PAST, D_ATTN), BF16)
            vst[:, 0:BAND_PAST, :] = jnp.zeros((nb, BAND_PAST, D_ATTN), BF16)
        if kv_slab is not None and not n_prev:
            layer, depth = kv_slab
            for other in [l for l in range(depth) if l != layer]:
                nk_ref[other] = jnp.zeros((nb, D_ATTN, tl), F32)
                nv_ref[other] = jnp.zeros((nb, D_ATTN, tl), F32)
        if srows > BAND_PAST + tl:
            pad = srows - BAND_PAST - tl
            kst[:, BAND_PAST + tl:, :] = jnp.zeros((nb, pad, D_ATTN), BF16)
            vst[:, BAND_PAST + tl:, :] = jnp.zeros((nb, pad, D_ATTN), BF16)

    x = x_ref[...].reshape(m, D_MODEL)
    h = _rms(x, gmix_ref[...]).astype(BF16)
    h_sc[...] = h
    u = _dot(h, win_ref[:, 0:D_POOL])
    q = _dot(h, win_ref[:, D_POOL:D_POOL + D_ATTN]) * (HEAD_DIM ** -0.5 * LOG2E)
    k = _dot(h, win_ref[:, D_POOL + D_ATTN:D_POOL + 2 * D_ATTN])
    v = _dot(h, win_ref[:, D_POOL + 2 * D_ATTN:D_POOL + 3 * D_ATTN])

    lane = lax.broadcasted_iota(jnp.int32, (m, D_ATTN), 1)
    first_head = (lane % LANES) < HEAD_DIM
    qa_sc[...] = jnp.where(first_head, q, 0.0).astype(BF16)
    qb_sc[...] = jnp.where(first_head, 0.0, q).astype(BF16)

    kst[:, BAND_PAST:BAND_PAST + tl, :] = k.astype(BF16).reshape(nb, tl, D_ATTN)
    vst[:, BAND_PAST:BAND_PAST + tl, :] = v.astype(BF16).reshape(nb, tl, D_ATTN)

    if kv_slab is None:
        nk_ref[...] = k.reshape(nb, tl, D_ATTN)
        nv_ref[...] = v.reshape(nb, tl, D_ATTN)
    else:
        for ref, new in ((nk_ref, k), (nv_ref, v)):
            new_t = new.T.reshape(nb, D_ATTN, tl)
            if n_prev:
                ref[...] = new_t
            else:
                ref[kv_slab[0]] = new_t

    row = lax.broadcasted_iota(jnp.int32, (tl, POOL_GRP), 0)
    for b in range(nb):
        est[b, POOL_HALO:POOL_HALO + tl, :] = u[b * tl:(b + 1) * tl, :]
        e = est[b]
        for g, w in enumerate(POOL_WINDOWS):
            eg = e[:, g * POOL_GRP:(g + 1) * POOL_GRP]
            s = eg
            sh = 1
            while sh < w:
                s = s + pltpu.roll(s, sh, 0)
                sh *= 2
            if pos0 >= POOL_PAD:
                mean = s[POOL_HALO:, :] * (1.0 / w)
            else:
                cnt = jnp.minimum(pos0 + t * tl + row + 1, w).astype(F32)
                mean = s[POOL_HALO:, :] / cnt
            dg = mean - eg[POOL_HALO:, :]
            d_sc[b * tl:(b + 1) * tl, g * POOL_GRP:(g + 1) * POOL_GRP] = dg.astype(BF16)
        tail = e[tl:tl + POOL_HALO, :]
        est[b, 0:POOL_HALO, :] = tail
        npool_ref[b] = tail

    n_grp = tl // grp
    olane = lax.broadcasted_iota(jnp.int32, (grp, LANES), 1)
    for g in range(n_grp):
        n_early = min(kband, -(-max(0, BAND_PAST - pos0 - g * grp) // LANES) * LANES)
        if n_early:
            kcol = lax.broadcasted_iota(jnp.int32, (1, n_early), 1)
            first_pos = pos0 + t * tl + g * grp - BAND_PAST
            kmask = jnp.where(kcol + first_pos >= 0, 0.0, NEG_INF).astype(F32)
        for hp in range(N_HEADS // HEADS_PER_VREG):
            cols = slice(hp * LANES, (hp + 1) * LANES)
            ss = []
            for b in range(nb):
                r0 = b * tl + g * grp
                qs = jnp.concatenate([qa_sc[r0:r0 + grp, cols], qb_sc[r0:r0 + grp, cols]], axis=0)
                k2 = kst[b, g * grp:g * grp + kband, cols]
                ss.append(lax.dot_general(qs, k2, (((1,), (1,)), ((), ())), preferred_element_type=F32))
            tab2 = [tab_ref[hp * HEADS_PER_VREG + j] for j in range(HEADS_PER_VREG)]
            s = jnp.concatenate(ss, axis=0) + jnp.concatenate(tab2 * nb, axis=0)
            if n_early == kband:
                s = s + kmask
            elif n_early:
                s = jnp.concatenate([s[:, :n_early] + kmask, s[:, n_early:]], axis=1)
            p = jnp.exp2(s - jnp.max(s, axis=-1, keepdims=True))
            inv = 1.0 / jnp.sum(p, axis=-1, keepdims=True)
            p = p.astype(BF16)
            for b in range(nb):
                r0 = b * tl + g * grp
                rows = slice(b * 2 * grp, (b + 1) * 2 * grp)
                v2 = vst[b, g * grp:g * grp + kband, cols]
                o = _dot(p[rows, :], v2) * inv[rows, :]
                att_sc[r0:r0 + grp, cols] = jnp.where(olane < HEAD_DIM, o[0:grp, :], o[grp:, :]).astype(BF16)

    ys = [_dot(d_sc[:, g * POOL_GRP:(g + 1) * POOL_GRP], wgrp_ref[g]) for g in range(len(POOL_WINDOWS))]
    y = (jnp.concatenate(ys, axis=1) * pscale_ref[...]).astype(BF16)
    g_off = D_POOL + 3 * D_ATTN
    gate_p = jax.nn.sigmoid(_dot(h_sc[...], win_ref[:, g_off:g_off + D_MODEL]) + bgate_ref[:, 0:D_MODEL])
    merged = gate_p * _dot(y, wpp_ref[...])
    gate_a = jax.nn.sigmoid(_dot(h_sc[...], win_ref[:, g_off + D_MODEL:g_off + 2 * D_MODEL])
                            + bgate_ref[:, D_MODEL:2 * D_MODEL])
    merged = (merged + gate_a * _dot(att_sc[...], wap_ref[...])).astype(BF16)
    xo = x_ref[...].reshape(m, D_MODEL) + _dot(merged, wout_ref[...])
    xo_ref[...] = xo.reshape(nb, tl, D_MODEL)

    if nt > 1:
        kst[:, 0:BAND_PAST, :] = kst[:, tl:tl + BAND_PAST, :]
        vst[:, 0:BAND_PAST, :] = vst[:, tl:tl + BAND_PAST, :]


def _key_band(grp):
    return -(-(BAND_PAST + grp) // LANES) * LANES


def _band_diag(rel_bias, grp):
    span = grp + _key_band(grp) - 1
    width = -(-span // LANES) * LANES
    rel = np.clip((grp - 1) - np.arange(span) + BAND_PAST, -REL_CLIP, REL_CLIP) + REL_CLIP
    diag = rel_bias[:, :, rel].astype(F32) * LOG2E
    return jnp.pad(diag, ((0, 0), (0, 0), (0, width - span)))


def _mixer(x, cache, lw, layer, prev_kv=None, stack_kv=False, *, tl, grp, pos0):
    bsz, seq, _ = x.shape
    depth = lw[0].shape[0]
    has_cache = cache is not None
    nb = bsz if has_cache else 1
    nt = seq // tl
    assert seq % tl == 0 and tl % grp == 0 and tl % (2 * SUBLANES) == 0
    assert not (has_cache and nt != 1)
    assert not (stack_kv and nb != 1)
    kband = _key_band(grp)
    srows = max(BAND_PAST + tl, (tl // grp - 1) * grp + kband)
    m = nb * tl

    row_spec = pl.BlockSpec((nb, tl, D_MODEL), lambda b, t: (b, t, 0))
    in_specs = [row_spec]
    args = [x]
    if has_cache:
        kv0_spec = pl.BlockSpec((None, nb, D_ATTN, BAND_PAST), lambda b, t: (layer, b, 0, 0),
                                pipeline_mode=pl.Buffered(1))
        in_specs += [pl.BlockSpec((None, nb, POOL_HALO, D_POOL), lambda b, t: (layer, b, 0, 0)), kv0_spec, kv0_spec]
        args += list(cache)
    in_specs += [_layer_spec(w, layer) for w in lw]
    args += list(lw)
    aliases = {}
    if prev_kv is not None:
        for n, arr in enumerate(prev_kv):
            aliases[len(args)] = 2 + n
            in_specs.append(pl.BlockSpec(memory_space=pl.ANY))
            args.append(arr)

    if not stack_kv:
        kv_shape = jax.ShapeDtypeStruct((bsz, tl, D_ATTN), F32)
        kv_spec = pl.BlockSpec((nb, tl, D_ATTN), lambda b, t: (b, 0, 0))
    else:
        kv_shape = jax.ShapeDtypeStruct((depth, bsz, D_ATTN, tl), F32)
        if prev_kv is None:
            kv_spec = pl.BlockSpec((depth, nb, D_ATTN, tl), lambda b, t: (0, b, 0, 0))
        else:
            kv_spec = pl.BlockSpec((None, nb, D_ATTN, tl), lambda b, t: (layer, b, 0, 0))
    out_shape = [jax.ShapeDtypeStruct((bsz, seq, D_MODEL), F32),
                 jax.ShapeDtypeStruct((bsz, POOL_HALO, D_POOL), F32), kv_shape, kv_shape]
    out_specs = [row_spec, pl.BlockSpec((nb, POOL_HALO, D_POOL), lambda b, t: (b, 0, 0)), kv_spec, kv_spec]
    scratch = [pltpu.VMEM((m, D_MODEL), BF16),
               pltpu.VMEM((nb, POOL_HALO + tl, D_POOL), F32),
               pltpu.VMEM((nb, srows, D_ATTN), BF16),
               pltpu.VMEM((nb, srows, D_ATTN), BF16),
               pltpu.VMEM((m, D_ATTN), BF16),
               pltpu.VMEM((m, D_ATTN), BF16),
               pltpu.VMEM((m, D_POOL), BF16),
               pltpu.VMEM((m, D_ATTN), BF16),
               pltpu.VMEM((N_HEADS, grp, kband), F32)]
    kern = functools.partial(_mixer_kernel, nb=nb, tl=tl, nt=nt, grp=grp, kband=kband,
                             pos0=pos0, has_cache=has_cache, kv_slab=(layer, depth) if stack_kv else None,
                             n_prev=len(aliases))
    return pl.pallas_call(
        kern,
        grid=(bsz // nb, nt),
        in_specs=in_specs,
        out_specs=out_specs,
        out_shape=out_shape,
        scratch_shapes=scratch,
        input_output_aliases=aliases,
        compiler_params=pltpu.CompilerParams(
            dimension_semantics=("arbitrary", "arbitrary"), vmem_limit_bytes=VMEM_LIMIT_BYTES),
        name="mixer_cached" if has_cache else "mixer",
    )(*args)


def _ffn_kernel(*refs, **static):
    _ffn_body(pl.program_id(1), *refs, **static)


def _ffn_body(t, x_ref, p_ref, conv0_ref, gffn_ref, wup_ref, wdw_ref, bdw_ref, wdown_ref,
              gple_ref, wple_ref, wpg_ref, gfin_ref,
              xo_ref, nconv_ref, cst, h_sc, y_sc, *, nb, tl, final_norm):
    m = nb * tl

    @pl.when(t == 0)
    def _init_state():
        cst[...] = conv0_ref[...]

    x = x_ref[...].reshape(m, D_MODEL)
    h_sc[...] = _rms(x, gffn_ref[...]).astype(BF16)
    row = lax.broadcasted_iota(jnp.int32, (tl, FF_CHUNK), 0)
    for j in range(N_FF_CHUNKS):
        cols = slice(j * FF_CHUNK, (j + 1) * FF_CHUNK)
        a_all = _dot(h_sc[...], wup_ref[:, cols])
        g_all = _dot(h_sc[...], wup_ref[:, D_FF + j * FF_CHUNK:D_FF + (j + 1) * FF_CHUNK])
        w0, w1, w2 = wdw_ref[0:1, cols], wdw_ref[1:2, cols], wdw_ref[2:3, cols]
        bias = bdw_ref[:, cols]
        for b in range(nb):
            a = a_all[b * tl:(b + 1) * tl, :]
            gate = g_all[b * tl:(b + 1) * tl, :]
            c6 = cst[b, CONV_HALO - 2:CONV_HALO - 1, cols]
            c7 = cst[b, CONV_HALO - 1:CONV_HALO, cols]
            a1 = jnp.where(row == 0, c7, pltpu.roll(a, 1, 0))
            a2 = jnp.where(row == 0, c6, jnp.where(row == 1, c7, pltpu.roll(a, 2, 0)))
            conv = bias + a2 * w0 + a1 * w1 + a * w2
            y_sc[b * tl:(b + 1) * tl, cols] = (jax.nn.gelu(conv) * gate).astype(BF16)
            tail = a[tl - CONV_HALO:tl, :]
            cst[b, :, cols] = tail
            nconv_ref[b, :, cols] = tail

    x2 = x_ref[...].reshape(m, D_MODEL) + _dot(y_sc[...], wdown_ref[...])
    hp = _rms(x2, gple_ref[...]).astype(BF16)
    gate = jax.nn.sigmoid(_dot(hp, wpg_ref[...]))
    pe = _dot(p_ref[...].reshape(m, PLE_DIM).astype(BF16), wple_ref[...])
    x3 = x2 + pe * gate
    if final_norm:
        x3 = _rms(x3, gfin_ref[...])
    xo_ref[...] = x3.reshape(nb, tl, D_MODEL)


def _ffn(x, p, conv0, lw, layer, *, tl, nb, final_norm, nested=False):
    bsz, seq, _ = x.shape
    nt = seq // tl
    assert seq % tl == 0 and bsz % nb == 0 and tl % CONV_HALO == 0
    m = nb * tl
    weights = list(lw)
    row_spec = pl.BlockSpec((nb, tl, D_MODEL), lambda b, t: (b, t, 0))
    in_specs = [row_spec,
                pl.BlockSpec((None, nb, tl, PLE_DIM), lambda b, t: (layer, b, t, 0)),
                pl.BlockSpec((None, nb, CONV_HALO, D_FF), lambda b, t: (layer, b, 0, 0))]
    in_specs += [_layer_spec(w, layer) for w in weights]
    out_specs = [row_spec, pl.BlockSpec((nb, CONV_HALO, D_FF), lambda b, t: (b, 0, 0))]
    out_shape = [jax.ShapeDtypeStruct((bsz, seq, D_MODEL), F32),
                 jax.ShapeDtypeStruct((bsz, CONV_HALO, D_FF), F32)]
    scratch = [pltpu.VMEM((nb, CONV_HALO, D_FF), F32),
               pltpu.VMEM((m, D_MODEL), BF16),
               pltpu.VMEM((m, D_FF), BF16)]
    if nested:
        n_in = len(in_specs)

        def tile(*refs):
            step_ref = refs[-1]
            step = step_ref[0]
            step_ref[0] = step + 1
            _ffn_body(lax.rem(step, nt), *refs[:-1], nb=nb, tl=tl, final_norm=final_norm)

        def whole(*refs):
            refs[-1][0] = 0
            pltpu.emit_pipeline(tile, grid=(bsz // nb, nt), in_specs=in_specs, out_specs=out_specs)(
                *refs[:n_in + 2], scratches=refs[n_in + 2:])

        return pl.pallas_call(
            whole,
            in_specs=[pl.BlockSpec(memory_space=pl.ANY)] * n_in,
            out_specs=[pl.BlockSpec(memory_space=pl.ANY)] * 2,
            out_shape=out_shape,
            scratch_shapes=scratch + [pltpu.SMEM((1,), jnp.int32)],
            compiler_params=pltpu.CompilerParams(vmem_limit_bytes=VMEM_LIMIT_BYTES),
            name="ffn_nested",
        )(x, p, conv0, *weights)
    kern = functools.partial(_ffn_kernel, nb=nb, tl=tl, final_norm=final_norm)
    return pl.pallas_call(
        kern,
        grid=(bsz // nb, nt),
        in_specs=in_specs,
        out_specs=out_specs,
        out_shape=out_shape,
        scratch_shapes=scratch,
        compiler_params=pltpu.CompilerParams(
            dimension_semantics=("arbitrary", "arbitrary"), vmem_limit_bytes=VMEM_LIMIT_BYTES),
        name="ffn",
    )(x, p, conv0, *weights)


PROMPT_TILE = 512
PROMPT_FFN_TILE = 1024
PROMPT_GROUP = 256


def kernel(x_prompt, x_sample, cache_pool, cache_k, cache_v, cache_ffn_conv, p_prompt, p_sample,
           g_mix, w_in, b_gate, w_pool_grp, pool_scale, rel_bias, w_pool_proj, w_attn_proj, w_out,
           g_ffn, w_up, w_dw, b_dw, w_down, g_ple, w_ple, w_ple_gate, g_final):
    depth = w_in.shape[0]
    bp = x_prompt.shape[0]
    bs, ls, _ = x_sample.shape

    bf = lambda a: a.astype(BF16)
    row2 = lambda a: a[:, None, :]
    mix_w = [row2(g_mix), bf(w_in), row2(b_gate), bf(w_pool_grp), row2(pool_scale), bf(w_pool_proj),
             bf(w_attn_proj), bf(w_out)]
    ffn_w = [row2(g_ffn), bf(w_up), w_dw, row2(b_dw), bf(w_down), row2(g_ple), bf(w_ple), bf(w_ple_gate),
             jnp.broadcast_to(g_final[None, None, :], (depth, 1, D_MODEL))]
    diag_p = _band_diag(rel_bias, PROMPT_GROUP)
    diag_s = _band_diag(rel_bias, ls)

    pool0_s = jnp.pad(cache_pool, ((0, 0), (0, 0), (POOL_HALO - POOL_PAD, 0), (0, 0)))
    k0_s = cache_k.transpose(0, 1, 3, 4, 2).reshape(depth, bs, D_ATTN, BAND_PAST)
    v0_s = cache_v.transpose(0, 1, 3, 4, 2).reshape(depth, bs, D_ATTN, BAND_PAST)
    conv0_s = jnp.pad(cache_ffn_conv, ((0, 0), (0, 0), (CONV_HALO - cache_ffn_conv.shape[2], 0), (0, 0)))
    conv0_p = jnp.zeros((depth, bp, CONV_HALO, D_FF), F32)

    xp, xs = x_prompt, x_sample
    outs_p, outs_s = [], []
    kv_p = None
    for i in range(depth):
        fin = i == depth - 1
        xp, npool, *kv_p = _mixer(xp, None, mix_w + [diag_p], i, kv_p, True,
                                  tl=PROMPT_TILE, grp=PROMPT_GROUP, pos0=0)
        xp, nconv = _ffn(xp, p_prompt, conv0_p, ffn_w, i, tl=PROMPT_FFN_TILE, nb=1, final_norm=fin, nested=True)
        outs_p.append((npool, nconv))
        xs, npool, nk, nv = _mixer(xs, (pool0_s, k0_s, v0_s), mix_w + [diag_s], i, tl=ls, grp=ls, pos0=PAST_LEN)
        xs, nconv = _ffn(xs, p_sample, conv0_s, ffn_w, i, tl=ls, nb=bs, final_norm=fin)
        outs_s.append((npool, nconv, nk, nv))

    def pool_conv(outs):
        return (jnp.stack([o[0][:, POOL_HALO - POOL_PAD:, :] for o in outs]),
                jnp.stack([o[1][:, CONV_HALO - 2:, :] for o in outs]))

    def by_head(kv_t):
        return kv_t.reshape(depth, bp, N_HEADS, HEAD_DIM, -1).transpose(0, 1, 4, 2, 3)

    pp, cp = pool_conv(outs_p)
    ps, cs = pool_conv(outs_s)
    ks = jnp.stack([o[2].reshape(bs, -1, N_HEADS, HEAD_DIM) for o in outs_s])
    vs = jnp.stack([o[3].reshape(bs, -1, N_HEADS, HEAD_DIM) for o in outs_s])
    return (xp, xs, pp, by_head(kv_p[0]), by_head(kv_p[1]), cp, ps, ks, vs, cs)
```

```python
import functools
import math

import jax
import jax.numpy as jnp
import numpy as np
from jax import lax
from jax.experimental import pallas as pl
from jax.experimental.pallas import tpu as pltpu

D_MODEL = 1024
CHUNK = 64
BAND_PAST = 512
D_ATTN = 512
PAST_LEN = 2048
HEAD_DIM = 64
N_HEADS = 8
REL_CLIP = 128
D_POOL = 512
POOL_WINDOWS = (2, 4, 8, 16)
POOL_GRP = 128
POOL_PAD = 15
D_FF = 2816
PLE_DIM = 256
EPS = 1e-6
NEG_INF = -1e30
LOG2E = math.log2(math.e)

LANES = 128
SUBLANES = 8
MXU_DIM = 256
VMEM_LIMIT_BYTES = 56 * 1024 * 1024

POOL_HALO = 16
CONV_HALO = SUBLANES
FF_CHUNK = MXU_DIM
N_FF_CHUNKS = D_FF // FF_CHUNK
HEADS_PER_VREG = LANES // HEAD_DIM

BF16 = jnp.bfloat16
F32 = jnp.float32


def _rms(x, g):
    return x * lax.rsqrt(jnp.mean(x * x, axis=-1, keepdims=True) + EPS) * g


def _dot(a, b):
    return jnp.dot(a, b, preferred_element_type=F32)


def _layer_spec(arr, layer):
    shape = arr.shape[1:]
    nd = len(shape)
    return pl.BlockSpec((None,) + shape, lambda b, t: (layer,) + (0,) * nd, pipeline_mode=pl.Buffered(1))


def _mixer_kernel(*refs, nb, tl, nt, grp, kband, pos0, has_cache, kv_slab, n_prev):
    it = iter(refs)
    x_ref = next(it)
    if has_cache:
        pool0_ref, k0_ref, v0_ref = next(it), next(it), next(it)
    gmix_ref, win_ref, bgate_ref, wgrp_ref, pscale_ref, wpp_ref, wap_ref, wout_ref, diag_ref = (
        next(it) for _ in range(9))
    for _ in range(n_prev):
        next(it)
    (xo_ref, npool_ref, nk_ref, nv_ref,
     h_sc, est, kst, vst, qa_sc, qb_sc, d_sc, att_sc, tab_ref) = it

    m = nb * tl
    t = pl.program_id(1)
    srows = kst.shape[1]

    @pl.when((pl.program_id(0) == 0) & (t == 0))
    def _build_band_table():
        width = diag_ref.shape[1]
        i = lax.broadcasted_iota(jnp.int32, (grp, kband), 0)
        j = lax.broadcasted_iota(jnp.int32, (grp, kband), 1)
        lo = jnp.bitwise_and(i, -CHUNK)
        valid = (j >= lo) & (j < lo + BAND_PAST + CHUNK) & (j < BAND_PAST + grp)
        for hd in range(N_HEADS):
            d = jnp.broadcast_to(diag_ref[hd:hd + 1, :], (grp, width))
            shifted = pltpu.roll(d, width - (grp - 1), 1, stride=1, stride_axis=0)
            tab_ref[hd] = jnp.where(valid, shifted[:, 0:kband], NEG_INF)

    @pl.when(t == 0)
    def _init_state():
        if has_cache:
            est[:, 0:POOL_HALO, :] = pool0_ref[...]
            for b in range(nb):
                kst[b, 0:BAND_PAST, :] = k0_ref[b].T.astype(BF16)
                vst[b, 0:BAND_PAST, :] = v0_ref[b].T.astype(BF16)
        else:
            est[:, 0:POOL_HALO, :] = jnp.zeros((nb, POOL_HALO, D_POOL), F32)
            kst[:, 0:BAND_PAST, :] = jnp.zeros((nb, BAND_PAST, D_ATTN), BF16)
            vst[:, 0:BAND_PAST, :] = jnp.zeros((nb, BAND_PAST, D_ATTN), BF16)
        if kv_slab is not None and not n_prev:
            layer, depth = kv_slab
            for other in [l for l in range(depth) if l != layer]:
                nk_ref[other] = jnp.zeros((nb, D_ATTN, tl), F32)
                nv_ref[other] = jnp.zeros((nb, D_ATTN, tl), F32)
        if srows > BAND_PAST + tl:
            pad = srows - BAND_PAST - tl
            kst[:, BAND_PAST + tl:, :] = jnp.zeros((nb, pad, D_ATTN), BF16)
            vst[:, BAND_PAST + tl:, :] = jnp.zeros((nb, pad, D_ATTN), BF16)

    x = x_ref[...].reshape(m, D_MODEL)
    h = _rms(x, gmix_ref[...]).astype(BF16)
    h_sc[...] = h
    u = _dot(h, win_ref[:, 0:D_POOL])
    q = _dot(h, win_ref[:, D_POOL:D_POOL + D_ATTN]) * (HEAD_DIM ** -0.5 * LOG2E)
    k = _dot(h, win_ref[:, D_POOL + D_ATTN:D_POOL + 2 * D_ATTN])
    v = _dot(h, win_ref[:, D_POOL + 2 * D_ATTN:D_POOL + 3 * D_ATTN])

    lane = lax.broadcasted_iota(jnp.int32, (m, D_ATTN), 1)
    first_head = (lane % LANES) < HEAD_DIM
    qa_sc[...] = jnp.where(first_head, q, 0.0).astype(BF16)
    qb_sc[...] = jnp.where(first_head, 0.0, q).astype(BF16)

    kst[:, BAND_PAST:BAND_PAST + tl, :] = k.astype(BF16).reshape(nb, tl, D_ATTN)
    vst[:, BAND_PAST:BAND_PAST + tl, :] = v.astype(BF16).reshape(nb, tl, D_ATTN)

    if kv_slab is None:
        nk_ref[...] = k.reshape(nb, tl, D_ATTN)
        nv_ref[...] = v.reshape(nb, tl, D_ATTN)
    else:
        for ref, new in ((nk_ref, k), (nv_ref, v)):
            new_t = new.T.reshape(nb, D_ATTN, tl)
            if n_prev:
                ref[...] = new_t
            else:
                ref[kv_slab[0]] = new_t

    row = lax.broadcasted_iota(jnp.int32, (tl, POOL_GRP), 0)
    for b in range(nb):
        est[b, POOL_HALO:POOL_HALO + tl, :] = u[b * tl:(b + 1) * tl, :]
        e = est[b]
        for g, w in enumerate(POOL_WINDOWS):
            eg = e[:, g * POOL_GRP:(g + 1) * POOL_GRP]
            s = eg
            sh = 1
            while sh < w:
                s = s + pltpu.roll(s, sh, 0)
                sh *= 2
            if pos0 >= POOL_PAD:
                mean = s[POOL_HALO:, :] * (1.0 / w)
            else:
                cnt = jnp.minimum(pos0 + t * tl + row + 1, w).astype(F32)
                mean = s[POOL_HALO:, :] / cnt
            dg = mean - eg[POOL_HALO:, :]
            d_sc[b * tl:(b + 1) * tl, g * POOL_GRP:(g + 1) * POOL_GRP] = dg.astype(BF16)
        tail = e[tl:tl + POOL_HALO, :]
        est[b, 0:POOL_HALO, :] = tail
        npool_ref[b] = tail

    n_grp = tl // grp
    olane = lax.broadcasted_iota(jnp.int32, (grp, LANES), 1)
    for g in range(n_grp):
        n_early = min(kband, -(-max(0, BAND_PAST - pos0 - g * grp) // LANES) * LANES)
        if n_early:
            kcol = lax.broadcasted_iota(jnp.int32, (1, n_early), 1)
            first_pos = pos0 + t * tl + g * grp - BAND_PAST
            kmask = jnp.where(kcol + first_pos >= 0, 0.0, NEG_INF).astype(F32)
        for hp in range(N_HEADS // HEADS_PER_VREG):
            cols = slice(hp * LANES, (hp + 1) * LANES)
            ss = []
            for b in range(nb):
                r0 = b * tl + g * grp
                qs = jnp.concatenate([qa_sc[r0:r0 + grp, cols], qb_sc[r0:r0 + grp, cols]], axis=0)
                k2 = kst[b, g * grp:g * grp + kband, cols]
                ss.append(lax.dot_general(qs, k2, (((1,), (1,)), ((), ())), preferred_element_type=F32))
            tab2 = [tab_ref[hp * HEADS_PER_VREG + j] for j in range(HEADS_PER_VREG)]
            s = jnp.concatenate(ss, axis=0) + jnp.concatenate(tab2 * nb, axis=0)
            if n_early == kband:
                s = s + kmask
            elif n_early:
                s = jnp.concatenate([s[:, :n_early] + kmask, s[:, n_early:]], axis=1)
            p = jnp.exp2(s - jnp.max(s, axis=-1, keepdims=True))
            inv = 1.0 / jnp.sum(p, axis=-1, keepdims=True)
            p = p.astype(BF16)
            for b in range(nb):
                r0 = b * tl + g * grp
                rows = slice(b * 2 * grp, (b + 1) * 2 * grp)
                v2 = vst[b, g * grp:g * grp + kband, cols]
                o = _dot(p[rows, :], v2) * inv[rows, :]
                att_sc[r0:r0 + grp, cols] = jnp.where(olane < HEAD_DIM, o[0:grp, :], o[grp:, :]).astype(BF16)

    ys = [_dot(d_sc[:, g * POOL_GRP:(g + 1) * POOL_GRP], wgrp_ref[g]) for g in range(len(POOL_WINDOWS))]
    y = (jnp.concatenate(ys, axis=1) * pscale_ref[...]).astype(BF16)
    g_off = D_POOL + 3 * D_ATTN
    gate_p = jax.nn.sigmoid(_dot(h_sc[...], win_ref[:, g_off:g_off + D_MODEL]) + bgate_ref[:, 0:D_MODEL])
    merged = gate_p * _dot(y, wpp_ref[...])
    gate_a = jax.nn.sigmoid(_dot(h_sc[...], win_ref[:, g_off + D_MODEL:g_off + 2 * D_MODEL])
                            + bgate_ref[:, D_MODEL:2 * D_MODEL])
    merged = (merged + gate_a * _dot(att_sc[...], wap_ref[...])).astype(BF16)
    xo = x_ref[...].reshape(m, D_MODEL) + _dot(merged, wout_ref[...])
    xo_ref[...] = xo.reshape(nb, tl, D_MODEL)

    if nt > 1:
        kst[:, 0:BAND_PAST, :] = kst[:, tl:tl + BAND_PAST, :]
        vst[:, 0:BAND_PAST, :] = vst[:, tl:tl + BAND_PAST, :]


def _key_band(grp):
    return -(-(BAND_PAST + grp) // LANES) * LANES


def _band_diag(rel_bias, grp):
    span = grp + _key_band(grp) - 1
    width = -(-span // LANES) * LANES
    rel = np.clip((grp - 1) - np.arange(span) + BAND_PAST, -REL_CLIP, REL_CLIP) + REL_CLIP
    diag = rel_bias[:, :, rel].astype(F32) * LOG2E
    return jnp.pad(diag, ((0, 0), (0, 0), (0, width - span)))


def _mixer(x, cache, lw, layer, prev_kv=None, stack_kv=False, *, tl, grp, pos0):
    bsz, seq, _ = x.shape
    depth = lw[0].shape[0]
    has_cache = cache is not None
    nb = bsz if has_cache else 1
    nt = seq // tl
    assert seq % tl == 0 and tl % grp == 0 and tl % (2 * SUBLANES) == 0
    assert not (has_cache and nt != 1)
    assert not (stack_kv and nb != 1)
    kband = _key_band(grp)
    srows = max(BAND_PAST + tl, (tl // grp - 1) * grp + kband)
    m = nb * tl

    row_spec = pl.BlockSpec((nb, tl, D_MODEL), lambda b, t: (b, t, 0))
    in_specs = [row_spec]
    args = [x]
    if has_cache:
        kv0_spec = pl.BlockSpec((None, nb, D_ATTN, BAND_PAST), lambda b, t: (layer, b, 0, 0),
                                pipeline_mode=pl.Buffered(1))
        in_specs += [pl.BlockSpec((None, nb, POOL_HALO, D_POOL), lambda b, t: (layer, b, 0, 0)), kv0_spec, kv0_spec]
        args += list(cache)
    in_specs += [_layer_spec(w, layer) for w in lw]
    args += list(lw)
    aliases = {}
    if prev_kv is not None:
        for n, arr in enumerate(prev_kv):
            aliases[len(args)] = 2 + n
            in_specs.append(pl.BlockSpec(memory_space=pl.ANY))
            args.append(arr)

    if not stack_kv:
        kv_shape = jax.ShapeDtypeStruct((bsz, tl, D_ATTN), F32)
        kv_spec = pl.BlockSpec((nb, tl, D_ATTN), lambda b, t: (b, 0, 0))
    else:
        kv_shape = jax.ShapeDtypeStruct((depth, bsz, D_ATTN, tl), F32)
        if prev_kv is None:
            kv_spec = pl.BlockSpec((depth, nb, D_ATTN, tl), lambda b, t: (0, b, 0, 0))
        else:
            kv_spec = pl.BlockSpec((None, nb, D_ATTN, tl), lambda b, t: (layer, b, 0, 0))
    out_shape = [jax.ShapeDtypeStruct((bsz, seq, D_MODEL), F32),
                 jax.ShapeDtypeStruct((bsz, POOL_HALO, D_POOL), F32), kv_shape, kv_shape]
    out_specs = [row_spec, pl.BlockSpec((nb, POOL_HALO, D_POOL), lambda b, t: (b, 0, 0)), kv_spec, kv_spec]
    scratch = [pltpu.VMEM((m, D_MODEL), BF16),
               pltpu.VMEM((nb, POOL_HALO + tl, D_POOL), F32),
               pltpu.VMEM((nb, srows, D_ATTN), BF16),
               pltpu.VMEM((nb, srows, D_ATTN), BF16),
               pltpu.VMEM((m, D_ATTN), BF16),
               pltpu.VMEM((m, D_ATTN), BF16),
               pltpu.VMEM((m, D_POOL), BF16),
               pltpu.VMEM((m, D_ATTN), BF16),
               pltpu.VMEM((N_HEADS, grp, kband), F32)]
    kern = functools.partial(_mixer_kernel, nb=nb, tl=tl, nt=nt, grp=grp, kband=kband,
                             pos0=pos0, has_cache=has_cache, kv_slab=(layer, depth) if stack_kv else None,
                             n_prev=len(aliases))
    return pl.pallas_call(
        kern,
        grid=(bsz // nb, nt),
        in_specs=in_specs,
        out_specs=out_specs,
        out_shape=out_shape,
        scratch_shapes=scratch,
        input_output_aliases=aliases,
        compiler_params=pltpu.CompilerParams(
            dimension_semantics=("arbitrary", "arbitrary"), vmem_limit_bytes=VMEM_LIMIT_BYTES),
        name="mixer_cached" if has_cache else "mixer",
    )(*args)


def _ffn_kernel(x_ref, p_ref, conv0_ref, gffn_ref, wup_ref, wdw_ref, bdw_ref, wdown_ref,
                gple_ref, wple_ref, wpg_ref, gfin_ref,
                xo_ref, nconv_ref, h_sc, y_sc, cst, *, nb, tl, final_norm):
    m = nb * tl
    t = pl.program_id(1)

    @pl.when(t == 0)
    def _init_state():
        cst[...] = conv0_ref[...]

    x = x_ref[...].reshape(m, D_MODEL)
    h_sc[...] = _rms(x, gffn_ref[...]).astype(BF16)
    row = lax.broadcasted_iota(jnp.int32, (tl, FF_CHUNK), 0)
    for j in range(N_FF_CHUNKS):
        cols = slice(j * FF_CHUNK, (j + 1) * FF_CHUNK)
        a_all = _dot(h_sc[...], wup_ref[:, cols])
        g_all = _dot(h_sc[...], wup_ref[:, D_FF + j * FF_CHUNK:D_FF + (j + 1) * FF_CHUNK])
        w0, w1, w2 = wdw_ref[0:1, cols], wdw_ref[1:2, cols], wdw_ref[2:3, cols]
        bias = bdw_ref[:, cols]
        for b in range(nb):
            a = a_all[b * tl:(b + 1) * tl, :]
            gate = g_all[b * tl:(b + 1) * tl, :]
            c6 = cst[b, CONV_HALO - 2:CONV_HALO - 1, cols]
            c7 = cst[b, CONV_HALO - 1:CONV_HALO, cols]
            a1 = jnp.where(row == 0, c7, pltpu.roll(a, 1, 0))
            a2 = jnp.where(row == 0, c6, jnp.where(row == 1, c7, pltpu.roll(a, 2, 0)))
            conv = bias + a2 * w0 + a1 * w1 + a * w2
            y_sc[b * tl:(b + 1) * tl, cols] = (jax.nn.gelu(conv) * gate).astype(BF16)
            tail = a[tl - CONV_HALO:tl, :]
            cst[b, :, cols] = tail
            nconv_ref[b, :, cols] = tail

    x2 = x_ref[...].reshape(m, D_MODEL) + _dot(y_sc[...], wdown_ref[...])
    hp = _rms(x2, gple_ref[...]).astype(BF16)
    gate = jax.nn.sigmoid(_dot(hp, wpg_ref[...]))
    pe = _dot(p_ref[...].reshape(m, PLE_DIM).astype(BF16), wple_ref[...])
    x3 = x2 + pe * gate
    if final_norm:
        x3 = _rms(x3, gfin_ref[...])
    xo_ref[...] = x3.reshape(nb, tl, D_MODEL)


def _ffn(x, p, conv0, lw, layer, *, tl, nb, final_norm):
    bsz, seq, _ = x.shape
    nt = seq // tl
    assert seq % tl == 0 and bsz % nb == 0 and tl % CONV_HALO == 0
    m = nb * tl
    weights = list(lw)
    row_spec = pl.BlockSpec((nb, tl, D_MODEL), lambda b, t: (b, t, 0))
    in_specs = [row_spec,
                pl.BlockSpec((None, nb, tl, PLE_DIM), lambda b, t: (layer, b, t, 0)),
                pl.BlockSpec((None, nb, CONV_HALO, D_FF), lambda b, t: (layer, b, 0, 0))]
    in_specs += [_layer_spec(w, layer) for w in weights]
    kern = functools.partial(_ffn_kernel, nb=nb, tl=tl, final_norm=final_norm)
    return pl.pallas_call(
        kern,
        grid=(bsz // nb, nt),
        in_specs=in_specs,
        out_specs=[row_spec, pl.BlockSpec((nb, CONV_HALO, D_FF), lambda b, t: (b, 0, 0))],
        out_shape=[jax.ShapeDtypeStruct((bsz, seq, D_MODEL), F32),
                   jax.ShapeDtypeStruct((bsz, CONV_HALO, D_FF), F32)],
        scratch_shapes=[pltpu.VMEM((m, D_MODEL), BF16),
                        pltpu.VMEM((m, D_FF), BF16),
                        pltpu.VMEM((nb, CONV_HALO, D_FF), F32)],
        compiler_params=pltpu.CompilerParams(
            dimension_semantics=("arbitrary", "arbitrary"), vmem_limit_bytes=VMEM_LIMIT_BYTES),
        name="ffn",
    )(x, p, conv0, *weights)


PROMPT_TILE = 512
PROMPT_FFN_TILE = 1024
PROMPT_GROUP = 256


def kernel(x_prompt, x_sample, cache_pool, cache_k, cache_v, cache_ffn_conv, p_prompt, p_sample,
           g_mix, w_in, b_gate, w_pool_grp, pool_scale, rel_bias, w_pool_proj, w_attn_proj, w_out,
           g_ffn, w_up, w_dw, b_dw, w_down, g_ple, w_ple, w_ple_gate, g_final):
    depth = w_in.shape[0]
    bp = x_prompt.shape[0]
    bs, ls, _ = x_sample.shape

    bf = lambda a: a.astype(BF16)
    row2 = lambda a: a[:, None, :]
    mix_w = [row2(g_mix), bf(w_in), row2(b_gate), bf(w_pool_grp), row2(pool_scale), bf(w_pool_proj),
             bf(w_attn_proj), bf(w_out)]
    ffn_w = [row2(g_ffn), bf(w_up), w_dw, row2(b_dw), bf(w_down), row2(g_ple), bf(w_ple), bf(w_ple_gate),
             jnp.broadcast_to(g_final[None, None, :], (depth, 1, D_MODEL))]
    diag_p = _band_diag(rel_bias, PROMPT_GROUP)
    diag_s = _band_diag(rel_bias, ls)

    pool0_s = jnp.pad(cache_pool, ((0, 0), (0, 0), (POOL_HALO - POOL_PAD, 0), (0, 0)))
    k0_s = cache_k.transpose(0, 1, 3, 4, 2).reshape(depth, bs, D_ATTN, BAND_PAST)
    v0_s = cache_v.transpose(0, 1, 3, 4, 2).reshape(depth, bs, D_ATTN, BAND_PAST)
    conv0_s = jnp.pad(cache_ffn_conv, ((0, 0), (0, 0), (CONV_HALO - cache_ffn_conv.shape[2], 0), (0, 0)))
    conv0_p = jnp.zeros((depth, bp, CONV_HALO, D_FF), F32)

    xp, xs = x_prompt, x_sample
    outs_p, outs_s = [], []
    kv_p = None
    for i in range(depth):
        fin = i == depth - 1
        xp, npool, *kv_p = _mixer(xp, None, mix_w + [diag_p], i, kv_p, True,
                                  tl=PROMPT_TILE, grp=PROMPT_GROUP, pos0=0)
        xp, nconv = _ffn(xp, p_prompt, conv0_p, ffn_w, i, tl=PROMPT_FFN_TILE, nb=1, final_norm=fin)
        outs_p.append((npool, nconv))
        xs, npool, nk, nv = _mixer(xs, (pool0_s, k0_s, v0_s), mix_w + [diag_s], i, tl=ls, grp=ls, pos0=PAST_LEN)
        xs, nconv = _ffn(xs, p_sample, conv0_s, ffn_w, i, tl=ls, nb=bs, final_norm=fin)
        outs_s.append((npool, nconv, nk, nv))

    def pool_conv(outs):
        return (jnp.stack([o[0][:, POOL_HALO - POOL_PAD:, :] for o in outs]),
                jnp.stack([o[1][:, CONV_HALO - 2:, :] for o in outs]))

    def by_head(kv_t):
        return kv_t.reshape(depth, bp, N_HEADS, HEAD_DIM, -1).transpose(0, 1, 4, 2, 3)

    pp, cp = pool_conv(outs_p)
    ps, cs = pool_conv(outs_s)
    ks = jnp.stack([o[2].reshape(bs, -1, N_HEADS, HEAD_DIM) for o in outs_s])
    vs = jnp.stack([o[3].reshape(bs, -1, N_HEADS, HEAD_DIM) for o in outs_s])
    return (xp, xs, pp, by_head(kv_p[0]), by_head(kv_p[1]), cp, ps, ks, vs, cs)
```
